```python
import math
import jax, jax.numpy as jnp
from jax import lax
import numpy as np


D_MODEL = 1024
BATCH = 16
SEQ = 2048
DEPTH = 1
DEC_BATCH = 4
DEC_SEQ = 4096
PAST_LEN = 128

ATT_HEAD_DIM = 64
ATT_HEADS_PER_GROUP = 8
DILATED_GROUPS = ((128, 1), (512, 4), (2048, 16))
ATT_HEADS = ATT_HEADS_PER_GROUP * len(DILATED_GROUPS)
ATT_WIDTH = ATT_HEADS * ATT_HEAD_DIM
ATT_OUT_WIDTH = ATT_HEADS_PER_GROUP * ATT_HEAD_DIM
ROPE_DIM = ATT_HEAD_DIM // 4
ROPE_THETA = 500000.0
DN_HEADS = 8
DN_HEAD_DIM = 128
DN_WIDTH = DN_HEADS * DN_HEAD_DIM
SHORT_CONV = 5
DN_CHUNK = 64
D_FF = 2816
FFN_CONV = 3
EPS = 1e-6
COL_ATT = 3 * ATT_WIDTH
COL_DN_QKV = 3 * DN_WIDTH
COL_DN_Z = DN_WIDTH
COL_DN_SMALL = 4 * DN_HEADS
COL_GATES = 2 * D_MODEL
IN_COLS = COL_ATT + COL_DN_QKV + COL_DN_Z + COL_DN_SMALL + COL_GATES

kernel_name = "hybrid_dilated_attn_gdn_encoder"


def rms_norm(x, g):
    xf = x.astype(jnp.float32)
    y = xf * lax.rsqrt(jnp.mean(xf * xf, axis=-1, keepdims=True) + EPS)
    return (y * g.astype(jnp.float32)).astype(x.dtype)


def l2_normalize(t):
    tf = t.astype(jnp.float32)
    return tf * lax.rsqrt(jnp.sum(tf * tf, axis=-1, keepdims=True) + EPS)


def centred_depthwise_conv(t, w):
    K = w.shape[0]
    r = K // 2
    S = t.shape[1]
    tp = jnp.pad(t, ((0, 0), (r, r), (0, 0)))
    out = tp[:, 0:S] * w[0]
    for i in range(1, K):
        out = out + tp[:, i:i + S] * w[i]
    return out


def partial_rope(t, pos):
    half = ROPE_DIM // 2
    inv = ROPE_THETA ** (-jnp.arange(half, dtype=jnp.float32) / half)
    ang = pos[:, None] * inv[None, :]
    cos = jnp.cos(ang)[None, :, None, :]
    sin = jnp.sin(ang)[None, :, None, :]
    tr = t[..., :ROPE_DIM].astype(jnp.float32)
    t1, t2 = tr[..., :half], tr[..., half:]
    rot = jnp.concatenate([t1 * cos - t2 * sin, t2 * cos + t1 * sin], axis=-1).astype(t.dtype)
    return jnp.concatenate([rot, t[..., ROPE_DIM:]], axis=-1)


def dilated_window_attention(q, k, v, window, dil):
    B, S, H, Dh = q.shape
    half = window // (2 * dil)
    blk = half
    L = S // dil
    nb = -(-L // blk)
    Lp = nb * blk

    def residue_major(t):
        t = t.reshape(B, L, dil, H, Dh).transpose(0, 2, 1, 3, 4)
        return jnp.pad(t, ((0, 0), (0, 0), (0, Lp - L), (0, 0), (0, 0)))

    def key_windows(t):
        t = jnp.pad(residue_major(t), ((0, 0), (0, 0), (blk, blk), (0, 0), (0, 0)))
        t = t.reshape(B, dil, nb + 2, blk, H, Dh)
        return jnp.concatenate([t[:, :, :-2], t[:, :, 1:-1], t[:, :, 2:]], axis=3)

    qs = residue_major(q).reshape(B, dil, nb, blk, H, Dh).astype(jnp.float32)
    ks = key_windows(k).astype(jnp.float32)
    vs = key_windows(v).astype(jnp.float32)
    qpos = jnp.arange(nb)[:, None] * blk + jnp.arange(blk)[None, :]
    kpos = (jnp.arange(nb)[:, None] - 1) * blk + jnp.arange(3 * blk)[None, :]
    diff = kpos[:, None, :] - qpos[:, :, None]
    valid = (jnp.abs(diff) <= half) & (kpos[:, None, :] >= 0) & (kpos[:, None, :] < L)
    s = jnp.einsum('bdnqhe,bdnkhe->bdnhqk', qs, ks) * (Dh ** -0.5)
    s = jnp.where(valid[None, None, :, None], s, -1e30)
    m = jnp.max(s, axis=-1, keepdims=True)
    p = jnp.exp(s - m)
    den = jnp.sum(p, axis=-1, keepdims=True)
    o = jnp.einsum('bdnhqk,bdnkhe->bdnqhe', p / den, vs)
    lse = (m + jnp.log(den))[..., 0]
    o = o.reshape(B, dil, Lp, H, Dh)[:, :, :L].transpose(0, 2, 1, 3, 4).reshape(B, S, H, Dh)
    lse = lse.transpose(0, 1, 2, 4, 3).reshape(B, dil, Lp, H)[:, :, :L].transpose(0, 2, 1, 3).reshape(B, S, H)
    return o, lse


def dilated_attention_branch(qkv, pos):
    B, S, _ = qkv.shape
    q, k, v = jnp.split(qkv, 3, axis=-1)
    q = partial_rope(q.reshape(B, S, ATT_HEADS, ATT_HEAD_DIM), pos)
    k = partial_rope(k.reshape(B, S, ATT_HEADS, ATT_HEAD_DIM), pos)
    v = v.reshape(B, S, ATT_HEADS, ATT_HEAD_DIM)
    outs, lses = [], []
    for gi, (window, dil) in enumerate(DILATED_GROUPS):
        sl = slice(gi * ATT_HEADS_PER_GROUP, (gi + 1) * ATT_HEADS_PER_GROUP)
        o, l = dilated_window_attention(q[:, :, sl], k[:, :, sl], v[:, :, sl], window, dil)
        outs.append(o)
        lses.append(l)
    wts = jax.nn.softmax(jnp.stack(lses, axis=0), axis=0)
    o = jnp.sum(wts[..., None] * jnp.stack(outs, axis=0), axis=0)
    return o.reshape(B, S, ATT_OUT_WIDTH).astype(qkv.dtype)


def gated_delta_rule(q, k, v, g, beta):
    B, S, H, Dk = q.shape
    Dv = v.shape[-1]
    C = DN_CHUNK
    N = S // C

    def chunks(t):
        t = t.reshape((B, N, C, H) + t.shape[3:])
        return jnp.moveaxis(t, 3, 1)

    qc, kc, vc = chunks(q), chunks(k), chunks(v)
    gc = jnp.cumsum(chunks(g), axis=-1)
    bc = chunks(beta)
    tril = jnp.tril(jnp.ones((C, C), dtype=bool))
    strict = jnp.tril(jnp.ones((C, C), dtype=bool), -1)
    gdiff = gc[..., :, None] - gc[..., None, :]
    dmask = jnp.where(tril, jnp.exp(jnp.where(tril, gdiff, 0.0)), 0.0)
    kb = kc * bc[..., None]
    vb = vc * bc[..., None]
    m_low = jnp.where(strict, jnp.einsum('bhnid,bhnjd->bhnij', kb, kc) * dmask, 0.0)
    eye = jnp.eye(C, dtype=jnp.float32)
    rhs = jnp.concatenate([vb, kb * jnp.exp(gc)[..., None]], axis=-1)
    sol = lax.linalg.triangular_solve(m_low + eye, rhs, left_side=True, lower=True, unit_diagonal=True)
    u, w = sol[..., :Dv], sol[..., Dv:]
    intra = jnp.where(tril, jnp.einsum('bhnid,bhnjd->bhnij', qc, kc) * dmask, 0.0)

    def step(state, xs):
        qi, ki, ui, wi, gi, ai = xs
        v_new = ui - jnp.einsum('bhck,bhkv->bhcv', wi, state)
        o = (jnp.einsum('bhck,bhkv->bhcv', qi * jnp.exp(gi)[..., None], state)
             + jnp.einsum('bhij,bhjv->bhiv', ai, v_new))
        glast = gi[..., -1]
        state = (state * jnp.exp(glast)[..., None, None]
                 + jnp.einsum('bhck,bhcv->bhkv', ki * jnp.exp(glast[..., None] - gi)[..., None], v_new))
        return state, o

    xs = tuple(jnp.moveaxis(t, 2, 0) for t in (qc, kc, u, w, gc, intra))
    state0 = jnp.zeros((B, H, Dk, Dv), dtype=jnp.float32)
    _, o = lax.scan(step, state0, xs)
    return o.transpose(1, 0, 3, 2, 4).reshape(B, S, H, Dv)


def deltanet_branch(qkv_raw, z, small, conv_qkv_w, a_log_f, a_log_b, dt_bias_f, dt_bias_b, out_norm_g):
    B, S, _ = qkv_raw.shape
    qkv = jax.nn.silu(centred_depthwise_conv(qkv_raw, conv_qkv_w))
    q, k, v = jnp.split(qkv, 3, axis=-1)
    q = l2_normalize(q.reshape(B, S, DN_HEADS, DN_HEAD_DIM)) * (DN_HEAD_DIM ** -0.5)
    k = l2_normalize(k.reshape(B, S, DN_HEADS, DN_HEAD_DIM))
    v = v.reshape(B, S, DN_HEADS, DN_HEAD_DIM).astype(jnp.float32)
    small = small.astype(jnp.float32)
    beta_f = jax.nn.sigmoid(small[..., 0:DN_HEADS])
    beta_b = jax.nn.sigmoid(small[..., DN_HEADS:2 * DN_HEADS])
    g_f = -jnp.exp(a_log_f.astype(jnp.float32)) * jax.nn.softplus(small[..., 2 * DN_HEADS:3 * DN_HEADS] + dt_bias_f.astype(jnp.float32))
    g_b = -jnp.exp(a_log_b.astype(jnp.float32)) * jax.nn.softplus(small[..., 3 * DN_HEADS:4 * DN_HEADS] + dt_bias_b.astype(jnp.float32))
    o_f = gated_delta_rule(q, k, v, g_f, beta_f)
    flip = lambda t: jnp.flip(t, axis=1)
    o_b = flip(gated_delta_rule(flip(q), flip(k), flip(v), flip(g_b), flip(beta_b)))
    o = rms_norm(o_f + o_b, out_norm_g)
    o = o * jax.nn.silu(z.reshape(B, S, DN_HEADS, DN_HEAD_DIM).astype(jnp.float32))
    return o.reshape(B, S, DN_WIDTH).astype(qkv_raw.dtype)


def trunk(x, norm_mix_g, w_in, conv_qkv_w, a_log_f, a_log_b, dt_bias_f, dt_bias_b, out_norm_g,
          w_branch_a, w_branch_b, w_out, norm_ffn_g, w_up, ffn_conv_w, ffn_conv_b, w_down, norm_final_g):
    B, S, _ = x.shape
    pos = jnp.arange(S, dtype=jnp.float32)
    for _layer in range(DEPTH):
        h = rms_norm(x, norm_mix_g)
        proj = h @ w_in
        o0 = COL_ATT
        o1 = o0 + COL_DN_QKV
        o2 = o1 + COL_DN_Z
        o3 = o2 + COL_DN_SMALL
        att_qkv = proj[..., :o0]
        dn_qkv = proj[..., o0:o1]
        dn_z = proj[..., o1:o2]
        dn_small = proj[..., o2:o3]
        gate_a = jax.nn.sigmoid(proj[..., o3:o3 + D_MODEL])
        gate_b = jax.nn.sigmoid(proj[..., o3 + D_MODEL:])
        y_a = dilated_attention_branch(att_qkv, pos) @ w_branch_a
        y_b = deltanet_branch(dn_qkv, dn_z, dn_small, conv_qkv_w, a_log_f, a_log_b, dt_bias_f, dt_bias_b, out_norm_g) @ w_branch_b
        x = x + (gate_a * y_a + gate_b * y_b) @ w_out
        h2 = rms_norm(x, norm_ffn_g)
        up = h2 @ w_up
        gate, val = jnp.split(up, 2, axis=-1)
        gate = centred_depthwise_conv(gate, ffn_conv_w) + ffn_conv_b
        x = x + (jax.nn.gelu(gate, approximate=False) * val) @ w_down
    return rms_norm(x, norm_final_g)


def setup_inputs(seed: int = 0) -> dict:
    key = jax.random.key(seed)
    ks = jax.random.split(key, 24)
    f32 = jnp.float32
    nrm = lambda k, shape, fan_in: jax.random.normal(k, shape, f32) * (fan_in ** -0.5)
    gain = lambda k, n: 1.0 + 0.02 * jax.random.normal(k, (n,), f32)

    def dt_bias(k):
        dt = jnp.exp(jax.random.uniform(k, (DN_HEADS,), f32, math.log(1e-3), math.log(1e-1)))
        return dt + jnp.log(-jnp.expm1(-dt))

    return {
        'x_prompt': jax.random.normal(ks[0], (BATCH, SEQ, D_MODEL), f32),
        'x_sample': jax.random.normal(ks[1], (DEC_BATCH, DEC_SEQ, D_MODEL), f32),
        'norm_mix_g': gain(ks[2], D_MODEL),
        'w_in': nrm(ks[3], (D_MODEL, IN_COLS), D_MODEL),
        'conv_qkv_w': nrm(ks[4], (SHORT_CONV, COL_DN_QKV), SHORT_CONV),
        'a_log_f': jnp.log(jax.random.uniform(ks[5], (DN_HEADS,), f32, 1.0, 16.0)),
        'a_log_b': jnp.log(jax.random.uniform(ks[6], (DN_HEADS,), f32, 1.0, 16.0)),
        'dt_bias_f': dt_bias(ks[7]),
        'dt_bias_b': dt_bias(ks[8]),
        'out_norm_g': gain(ks[9], DN_HEAD_DIM),
        'w_branch_a': nrm(ks[10], (ATT_OUT_WIDTH, D_MODEL), ATT_OUT_WIDTH),
        'w_branch_b': nrm(ks[11], (DN_WIDTH, D_MODEL), DN_WIDTH),
        'w_out': nrm(ks[12], (D_MODEL, D_MODEL), D_MODEL),
        'norm_ffn_g': gain(ks[13], D_MODEL),
        'w_up': nrm(ks[14], (D_MODEL, 2 * D_FF), D_MODEL),
        'ffn_conv_w': nrm(ks[15], (FFN_CONV, D_FF), FFN_CONV),
        'ffn_conv_b': 0.02 * jax.random.normal(ks[16], (D_FF,), f32),
        'w_down': nrm(ks[17], (D_FF, D_MODEL), D_FF),
        'norm_final_g': gain(ks[18], D_MODEL),
    }


def reference(x_prompt, x_sample, norm_mix_g, w_in, conv_qkv_w, a_log_f, a_log_b, dt_bias_f, dt_bias_b,
              out_norm_g, w_branch_a, w_branch_b, w_out, norm_ffn_g, w_up, ffn_conv_w, ffn_conv_b, w_down,
              norm_final_g):
    y_prompt = trunk(x_prompt, norm_mix_g, w_in, conv_qkv_w, a_log_f, a_log_b, dt_bias_f, dt_bias_b, out_norm_g,
                     w_branch_a, w_branch_b, w_out, norm_ffn_g, w_up, ffn_conv_w, ffn_conv_b, w_down, norm_final_g)
    y_sample = trunk(x_sample, norm_mix_g, w_in, conv_qkv_w, a_log_f, a_log_b, dt_bias_f, dt_bias_b, out_norm_g,
                     w_branch_a, w_branch_b, w_out, norm_ffn_g, w_up, ffn_conv_w, ffn_conv_b, w_down, norm_final_g)
    return (y_prompt, y_sample)
```

```python
import functools

import jax
import jax.numpy as jnp
from jax import lax
from jax.experimental import pallas as pl
from jax.experimental.pallas import tpu as pltpu

F32 = jnp.float32
BF16 = jnp.bfloat16
HIGHEST = lax.Precision.HIGHEST

D_MODEL = 1024
ATT_HEAD_DIM = 64
ATT_HEADS_PER_GROUP = 8
DILATIONS = (1, 4, 16)
ATT_HALF = 64
ATT_GROUP_WIDTH = ATT_HEADS_PER_GROUP * ATT_HEAD_DIM
ATT_WIDTH = len(DILATIONS) * ATT_GROUP_WIDTH
ROPE_DIM = ATT_HEAD_DIM // 4
ROPE_THETA = 500000.0
DN_HEADS = 8
DN_HEAD_DIM = 128
DN_WIDTH = DN_HEADS * DN_HEAD_DIM
SHORT_CONV = 5
DN_CHUNK = 64
D_FF = 2816
EPS = 1e-6
NEG_INF = -1e30

LANES = 128
MIB = 1024 * 1024

COL_GATE_A = 0
COL_GATE_B = 1024
COL_Z = 2048
COL_DN_Q = 3072
COL_DN_K = 4096
COL_DN_V = 5120
COL_ATT_Q = 6144
COL_ATT_K = COL_ATT_Q + ATT_WIDTH
COL_ATT_V = COL_ATT_K + ATT_WIDTH
COL_SMALL = COL_ATT_V + ATT_WIDTH
PROJ_COLS = COL_SMALL + LANES


def _params(semantics, vmem_mib):
    return pltpu.CompilerParams(dimension_semantics=semantics, vmem_limit_bytes=vmem_mib * MIB)


def _mm(a, b):
    return jnp.dot(a.astype(BF16), b.astype(BF16), preferred_element_type=F32)


def _mm_nt(a, b):
    return lax.dot_general(a.astype(BF16), b.astype(BF16), (((1,), (1,)), ((), ())),
                           preferred_element_type=F32)


def _norm_matmul_kernel(x_ref, g_ref, w_ref, o_ref, h_ref, *, row_chunk):
    @pl.when(pl.program_id(1) == 0)
    def _():
        def body(c, carry):
            r0 = pl.multiple_of(c * row_chunk, row_chunk)
            x = x_ref[pl.ds(r0, row_chunk), :]
            ms = jnp.mean(x * x, axis=-1, keepdims=True)
            h_ref[pl.ds(r0, row_chunk), :] = (x * lax.rsqrt(ms + EPS) * g_ref[...]).astype(BF16)
            return carry
        lax.fori_loop(0, x_ref.shape[0] // row_chunk, body, 0)

    o_ref[...] = jnp.dot(h_ref[...], w_ref[...], preferred_element_type=F32)


def _norm_matmul(x2d, gain, w_bf16, *, tm, tn):
    T, K = x2d.shape
    N = w_bf16.shape[1]
    return pl.pallas_call(
        functools.partial(_norm_matmul_kernel, row_chunk=128),
        grid=(T // tm, N // tn),
        in_specs=[
            pl.BlockSpec((tm, K), lambda i, j: (i, 0)),
            pl.BlockSpec((1, K), lambda i, j: (0, 0)),
            pl.BlockSpec((K, tn), lambda i, j: (0, j)),
        ],
        out_specs=pl.BlockSpec((tm, tn), lambda i, j: (i, j)),
        out_shape=jax.ShapeDtypeStruct((T, N), F32),
        scratch_shapes=[pltpu.VMEM((tm, K), BF16)],
        compiler_params=_params(("parallel", "arbitrary"), 48),
    )(x2d, gain.reshape(1, K), w_bf16)


ATT_QB = 128


def _attn_group(q_ref, k_ref, v_ref, cos_ref, sa_ref, sb_ref, qs, ks, vs, acc_ref, m_ref, l_ref,
                *, S, dil, first):
    L = S // dil
    rc = min(L, 256)
    nlc = L // rc

    def rows_of(start, n):
        return pl.ds(start, n, stride=dil) if dil > 1 else pl.ds(start, n)

    def pre(idx, carry):
        r = idx // nlc
        lc = idx % nlc
        rows = rows_of(r + dil * lc * rc, rc)
        dst = pl.ds(pl.multiple_of(r * L + lc * rc, rc), rc)
        cs = cos_ref[rows, :]
        sa = sa_ref[rows, :]
        sb = sb_ref[rows, :]

        def rope(t):
            return t * cs + pltpu.roll(t, LANES - ROPE_DIM // 2, 1) * sa + pltpu.roll(t, ROPE_DIM // 2, 1) * sb

        q = rope(q_ref[0, rows, :]) * (ATT_HEAD_DIM ** -0.5)
        k = rope(k_ref[0, rows, :])
        v = v_ref[0, rows, :]
        for hh in range(2):
            sl = slice(hh * ATT_HEAD_DIM, (hh + 1) * ATT_HEAD_DIM)
            qs[hh, dst, :] = q[:, sl].astype(BF16)
            ks[hh, dst, :] = k[:, sl].astype(BF16)
            vs[hh, dst, :] = v[:, sl].astype(BF16)
        return carry

    lax.fori_loop(0, dil * nlc, pre, 0)

    nb = L // ATT_QB
    nk = min(2 * ATT_QB, L)
    col = lax.broadcasted_iota(jnp.int32, (ATT_QB, nk), 1)
    row = lax.broadcasted_iota(jnp.int32, (ATT_QB, nk), 0)

    def blk(idx, carry):
        r = idx // nb
        n = idx % nb
        q0 = n * ATT_QB
        start = jnp.clip(q0 - ATT_HALF, 0, L - nk)
        valid = jnp.abs(col - row + (start - q0)) <= ATT_HALF
        rows = rows_of(r + dil * q0, ATT_QB)
        qsl = pl.ds(pl.multiple_of(r * L + q0, ATT_QB), ATT_QB)
        ksl = pl.ds(pl.multiple_of(r * L + start, ATT_HALF), nk)
        for hh in range(2):
            s = lax.dot_general(qs[hh, qsl, :], ks[hh, ksl, :], (((1,), (1,)), ((), ())),
                                preferred_element_type=F32)
            s = jnp.where(valid, s, NEG_INF)
            m = jnp.max(s, axis=1, keepdims=True)
            p = jnp.exp(s - m)
            l = jnp.sum(p, axis=1, keepdims=True)
            pv = jnp.dot(p.astype(BF16), vs[hh, ksl, :], preferred_element_type=F32)
            if first:
                m_ref[hh, rows, :] = m
                l_ref[hh, rows, :] = l
                acc_ref[hh, rows, :] = pv
            else:
                m_old = m_ref[hh, rows, :]
                m_new = jnp.maximum(m_old, m)
                a = jnp.exp(m_old - m_new)
                b = jnp.exp(m - m_new)
                m_ref[hh, rows, :] = m_new
                l_ref[hh, rows, :] = a * l_ref[hh, rows, :] + b * l
                acc_ref[hh, rows, :] = a * acc_ref[hh, rows, :] + b * pv
        return carry

    lax.fori_loop(0, dil * nb, blk, 0)


def _attn_kernel(q_ref, k_ref, v_ref, cos_ref, sa_ref, sb_ref, o_ref,
                 qs, ks, vs, acc_ref, m_ref, l_ref, *, S):
    g = pl.program_id(2)
    for gi, dil in enumerate(DILATIONS):
        @pl.when(g == gi)
        def _(gi=gi, dil=dil):
            _attn_group(q_ref, k_ref, v_ref, cos_ref, sa_ref, sb_ref, qs, ks, vs, acc_ref, m_ref, l_ref,
                        S=S, dil=dil, first=(gi == 0))

    @pl.when(g == len(DILATIONS) - 1)
    def _():
        rc = 256

        def fin(c, carry):
            sl = pl.ds(pl.multiple_of(c * rc, rc), rc)
            o = jnp.concatenate([acc_ref[hh, sl, :] / l_ref[hh, sl, :] for hh in range(2)], axis=1)
            o_ref[0, sl, :] = o.astype(o_ref.dtype)
            return carry

        lax.fori_loop(0, S // rc, fin, 0)


def _attention(proj3, cos_t, sa_t, sb_t):
    B, S, _ = proj3.shape
    ng = len(DILATIONS)

    def col_spec(col0):
        base = col0 // LANES
        per_group = ATT_GROUP_WIDTH // LANES
        return pl.BlockSpec((1, S, LANES), lambda b, j, g: (b, 0, base + g * per_group + j))

    tab_spec = pl.BlockSpec((S, LANES), lambda b, j, g: (0, 0))
    head_scr = pltpu.VMEM((2, S, ATT_HEAD_DIM), BF16)
    return pl.pallas_call(
        functools.partial(_attn_kernel, S=S),
        grid=(B, ATT_GROUP_WIDTH // LANES, ng),
        in_specs=[col_spec(COL_ATT_Q), col_spec(COL_ATT_K), col_spec(COL_ATT_V), tab_spec, tab_spec, tab_spec],
        out_specs=pl.BlockSpec((1, S, LANES), lambda b, j, g: (b, 0, j)),
        out_shape=jax.ShapeDtypeStruct((B, S, ATT_GROUP_WIDTH), BF16),
        scratch_shapes=[head_scr, head_scr, head_scr,
                        pltpu.VMEM((2, S, ATT_HEAD_DIM), F32),
                        pltpu.VMEM((2, S, 1), F32),
                        pltpu.VMEM((2, S, 1), F32)],
        compiler_params=_params(("parallel", "parallel", "arbitrary"), 48),
    )(proj3, proj3, proj3, cos_t, sa_t, sb_t)


def _rope_tables(S):
    half = ROPE_DIM // 2
    inv = ROPE_THETA ** (-jnp.arange(half, dtype=F32) / half)
    ang = jnp.arange(S, dtype=F32)[:, None] * inv[None, :]
    cos, sin = jnp.cos(ang), jnp.sin(ang)
    pad = ATT_HEAD_DIM - ROPE_DIM
    ones = jnp.ones((S, pad), F32)
    zeros = jnp.zeros((S, pad), F32)
    zh = jnp.zeros((S, half), F32)
    cos_h = jnp.concatenate([cos, cos, ones], axis=1)
    sa_h = jnp.concatenate([-sin, zh, zeros], axis=1)
    sb_h = jnp.concatenate([zh, sin, zeros], axis=1)
    tile = lambda t: jnp.concatenate([t, t], axis=1)
    return tile(cos_h), tile(sa_h), tile(sb_h)


DN_UNIT = 2 * DN_CHUNK


def _dn_kernel(q_ref, k_ref, v_ref, z_ref, sm_ref, cwq_ref, cwk_ref, cwv_ref, par_ref, ng_ref, o_ref,
               xp, qn, kn, vn, u_s, w_s, qg_s, kg_s, in_s, dec_s, oacc, *, S):
    h = pl.program_id(1)
    rc = 256
    halo = 8

    xp[0:halo, :] = jnp.zeros((halo, LANES), F32)
    xp[S + halo:S + 2 * halo, :] = jnp.zeros((halo, LANES), F32)
    for src, cw, dst, mode in ((q_ref, cwq_ref, qn, "q"), (k_ref, cwk_ref, kn, "k"), (v_ref, cwv_ref, vn, "v")):
        def cp(c, carry, src=src):
            r0 = pl.multiple_of(c * rc, rc)
            xp[pl.ds(r0 + halo, rc), :] = src[0, pl.ds(r0, rc), :]
            return carry

        lax.fori_loop(0, S // rc, cp, 0)

        def cv(c, carry, cw=cw, dst=dst, mode=mode):
            r0 = pl.multiple_of(c * rc, rc)
            win = xp[pl.ds(r0, rc + 2 * halo), :]
            off = halo - SHORT_CONV // 2
            y = win[off:off + rc] * cw[0:1, :]
            for i in range(1, SHORT_CONV):
                y = y + win[off + i:off + i + rc] * cw[i:i + 1, :]
            y = y * jax.nn.sigmoid(y)
            if mode != "v":
                y = y * lax.rsqrt(jnp.sum(y * y, axis=-1, keepdims=True) + EPS)
            if mode == "q":
                y = y * (DN_HEAD_DIM ** -0.5)
            dst[pl.ds(r0, rc), :] = y
            return carry

        lax.fori_loop(0, S // rc, cv, 0)

    lane = lax.broadcasted_iota(jnp.int32, (DN_UNIT, DN_UNIT), 1)
    rowi = lax.broadcasted_iota(jnp.int32, (DN_UNIT, DN_UNIT), 0)
    same = (rowi >= DN_CHUNK) == (lane >= DN_CHUNK)
    eye = jnp.where(lane == rowi, 1.0, 0.0)
    a_exp = jnp.exp(par_ref[0:1, :])
    dt_bias = par_ref[1:2, :]
    n_units = S // DN_UNIT

    def unit(c, carry):
        r0 = pl.multiple_of(c * DN_UNIT, DN_UNIT)
        rows = pl.ds(r0, DN_UNIT)
        sm = sm_ref[0, rows, :]
        beta_all = jax.nn.sigmoid(sm)
        g_all = -a_exp * jax.nn.softplus(sm + dt_bias)
        q = qn[rows, :]
        k = kn[rows, :]
        v = vn[rows, :]
        qk = _mm_nt(q, k)
        for d in range(2):
            if d == 0:
                tri = same & (lane <= rowi)
                strict = same & (lane < rowi)
            else:
                tri = same & (lane >= rowi)
                strict = same & (lane > rowi)
            beta = jnp.sum(jnp.where(lane == h + DN_HEADS * d, beta_all, 0.0), axis=1, keepdims=True)
            g = jnp.sum(jnp.where(lane == h + DN_HEADS * (2 + d), g_all, 0.0), axis=1, keepdims=True)
            gb = jnp.broadcast_to(g, (DN_UNIT, DN_UNIT))
            gc = jnp.dot(jnp.where(tri, 1.0, 0.0), gb, precision=HIGHEST, preferred_element_type=F32)
            last0, last1 = (DN_CHUNK - 1, DN_UNIT - 1) if d == 0 else (0, DN_CHUNK)
            gl = jnp.where(rowi < DN_CHUNK, gc[last0:last0 + 1, :], gc[last1:last1 + 1, :])
            dm = jnp.where(tri, jnp.exp(jnp.where(tri, gc - gc.T, 0.0)), 0.0)
            kb = k * beta
            vb = v * beta
            nm = jnp.where(strict, -(_mm_nt(kb, k) * dm), 0.0)
            x = eye + nm
            p = nm
            for _ in range(5):
                p = _mm(p, p)
                x = x + _mm(x, p)
            eg = jnp.exp(gc)
            u_s[d, rows, :] = _mm(x, vb)
            w_s[d, rows, :] = _mm(x, kb * eg).astype(BF16)
            qg_s[d, rows, :] = (q * eg).astype(BF16)
            kg_s[d, rows, :] = (k * jnp.exp(gl - gc)).astype(BF16)
            in_s[d, rows, :] = (qk * dm).astype(BF16)
            dec_s[d, rows, :] = jnp.exp(gl)
        return carry

    lax.fori_loop(0, n_units, unit, 0)

    def zero(c, carry):
        oacc[pl.ds(pl.multiple_of(c * rc, rc), rc), :] = jnp.zeros((rc, LANES), F32)
        return carry

    lax.fori_loop(0, S // rc, zero, 0)

    def step(d, r0, half, state):
        rows = pl.ds(pl.multiple_of(r0 + half * DN_CHUNK, DN_CHUNK), DN_CHUNK)
        sb = state.astype(BF16)
        v_new = u_s[d, rows, :] - jnp.dot(w_s[d, rows, :], sb, preferred_element_type=F32)
        vb = v_new.astype(BF16)
        intra = in_s[d, rows, half * DN_CHUNK:(half + 1) * DN_CHUNK]
        o = (jnp.dot(qg_s[d, rows, :], sb, preferred_element_type=F32)
             + jnp.dot(intra, vb, preferred_element_type=F32))
        oacc[rows, :] += o
        dec = dec_s[d, pl.ds(pl.multiple_of(r0 + half * DN_CHUNK, DN_CHUNK), 1), :]
        kv = lax.dot_general(kg_s[d, rows, :], vb, (((0,), (0,)), ((), ())), preferred_element_type=F32)
        return state * dec + kv

    def rec(t, carry):
        sf, sb = carry
        rf = t * DN_UNIT
        rb = (n_units - 1 - t) * DN_UNIT
        sf = step(0, rf, 0, sf)
        sb = step(1, rb, 1, sb)
        sf = step(0, rf, 1, sf)
        sb = step(1, rb, 0, sb)
        return sf, sb

    zero_state = jnp.zeros((DN_HEAD_DIM, DN_HEAD_DIM), F32)
    lax.fori_loop(0, n_units, rec, (zero_state, zero_state))

    def fin(c, carry):
        sl = pl.ds(pl.multiple_of(c * rc, rc), rc)
        o = oacc[sl, :]
        o = o * lax.rsqrt(jnp.mean(o * o, axis=-1, keepdims=True) + EPS) * ng_ref[...]
        z = z_ref[0, sl, :]
        o_ref[0, sl, :] = (o * (z * jax.nn.sigmoid(z))).astype(o_ref.dtype)
        return carry

    lax.fori_loop(0, S // rc, fin, 0)


def _deltanet(proj3, conv_w, par, norm_g):
    B, S, _ = proj3.shape

    def col_spec(col0):
        base = col0 // LANES
        return pl.BlockSpec((1, S, LANES), lambda b, h: (b, 0, base + h))

    def conv_spec(part):
        return pl.BlockSpec((SHORT_CONV, LANES), lambda b, h: (0, part * DN_HEADS + h))

    seq_f32 = pltpu.VMEM((S, LANES), F32)
    dir_f32 = pltpu.VMEM((2, S, LANES), F32)
    dir_bf16 = pltpu.VMEM((2, S, LANES), BF16)
    return pl.pallas_call(
        functools.partial(_dn_kernel, S=S),
        grid=(B, DN_HEADS),
        in_specs=[col_spec(COL_DN_Q), col_spec(COL_DN_K), col_spec(COL_DN_V), col_spec(COL_Z),
                  pl.BlockSpec((1, S, LANES), lambda b, h: (b, 0, COL_SMALL // LANES)),
                  conv_spec(0), conv_spec(1), conv_spec(2),
                  pl.BlockSpec((8, LANES), lambda b, h: (0, 0)),
                  pl.BlockSpec((1, LANES), lambda b, h: (0, 0))],
        out_specs=pl.BlockSpec((1, S, LANES), lambda b, h: (b, 0, h)),
        out_shape=jax.ShapeDtypeStruct((B, S, DN_WIDTH), BF16),
        scratch_shapes=[pltpu.VMEM((S + 16, LANES), F32), seq_f32, seq_f32, seq_f32,
                        dir_f32, dir_bf16, dir_bf16, dir_bf16, dir_bf16, dir_f32, seq_f32],
        compiler_params=_params(("parallel", "parallel"), 56),
    )(proj3, proj3, proj3, proj3, proj3, conv_w, conv_w, conv_w, par, norm_g.reshape(1, LANES))


def _mix_kernel(x_ref, oa_ref, ob_ref, ga_ref, gb_ref, wa_ref, wb_ref, wo_ref, o_ref):
    ya = jnp.dot(oa_ref[...], wa_ref[...], preferred_element_type=F32)
    yb = jnp.dot(ob_ref[...], wb_ref[...], preferred_element_type=F32)
    mix = jax.nn.sigmoid(ga_ref[...]) * ya + jax.nn.sigmoid(gb_ref[...]) * yb
    o_ref[...] = x_ref[...] + jnp.dot(mix.astype(BF16), wo_ref[...], preferred_element_type=F32)


def _mix(x2d, o_att, o_dn, proj, wa, wb, wo, *, tm):
    T = x2d.shape[0]
    row = lambda width, cb=0: pl.BlockSpec((tm, width), lambda i: (i, cb))
    full = lambda a: pl.BlockSpec(a.shape, lambda i: (0, 0))
    return pl.pallas_call(
        _mix_kernel,
        grid=(T // tm,),
        in_specs=[row(D_MODEL), row(ATT_GROUP_WIDTH), row(DN_WIDTH),
                  row(D_MODEL, COL_GATE_A // D_MODEL), row(D_MODEL, COL_GATE_B // D_MODEL),
                  full(wa), full(wb), full(wo)],
        out_specs=row(D_MODEL),
        out_shape=jax.ShapeDtypeStruct((T, D_MODEL), F32),
        compiler_params=_params(("parallel",), 48),
    )(x2d, o_att, o_dn, proj, proj, wa, wb, wo)


FFN_KC = 256


def _ffn_out_kernel(g_ref, v_ref, gp_ref, gn_ref, x_ref, cw_ref, cb_ref, wd_ref, nf_ref, o_ref, act_ref,
                    *, tiles_per_seq):
    i = pl.program_id(0)
    tm = g_ref.shape[0]
    pos = i % tiles_per_seq
    keep_prev = jnp.where(pos == 0, 0.0, 1.0)
    keep_next = jnp.where(pos == tiles_per_seq - 1, 0.0, 1.0)
    row = lax.broadcasted_iota(jnp.int32, (tm, FFN_KC), 0)
    for c in range(D_FF // FFN_KC):
        sl = slice(c * FFN_KC, (c + 1) * FFN_KC)
        g = g_ref[:, sl]
        prev_row = gp_ref[7:8, sl] * keep_prev
        next_row = gn_ref[0:1, sl] * keep_next
        g_prev = jnp.where(row == 0, prev_row, pltpu.roll(g, 1, 0))
        g_next = jnp.where(row == tm - 1, next_row, pltpu.roll(g, tm - 1, 0))
        conv = g_prev * cw_ref[0:1, sl] + g * cw_ref[1:2, sl] + g_next * cw_ref[2:3, sl] + cb_ref[:, sl]
        gelu = 0.5 * conv * (1.0 + lax.erf(conv * (2.0 ** -0.5)))
        act_ref[:, sl] = (gelu * v_ref[:, sl]).astype(BF16)
    x2 = x_ref[...] + jnp.dot(act_ref[...], wd_ref[...], preferred_element_type=F32)
    ms = jnp.mean(x2 * x2, axis=-1, keepdims=True)
    o_ref[...] = x2 * lax.rsqrt(ms + EPS) * nf_ref[...]


def _ffn_out(up, x1, conv_w, conv_b, wd, norm_g, *, tm, S):
    T = x1.shape[0]
    tiles_per_seq = S // tm
    hb = tm // 8
    last_hb = T // 8 - 1
    return pl.pallas_call(
        functools.partial(_ffn_out_kernel, tiles_per_seq=tiles_per_seq),
        grid=(T // tm,),
        in_specs=[
            pl.BlockSpec((tm, D_FF), lambda i: (i, 0)),
            pl.BlockSpec((tm, D_FF), lambda i: (i, 1)),
            pl.BlockSpec((8, D_FF), lambda i: (jnp.maximum(i * hb - 1, 0), 0)),
            pl.BlockSpec((8, D_FF), lambda i: (jnp.minimum((i + 1) * hb, last_hb), 0)),
            pl.BlockSpec((tm, D_MODEL), lambda i: (i, 0)),
            pl.BlockSpec((3, D_FF), lambda i: (0, 0)),
            pl.BlockSpec((1, D_FF), lambda i: (0, 0)),
            pl.BlockSpec((D_FF, D_MODEL), lambda i: (0, 0)),
            pl.BlockSpec((1, D_MODEL), lambda i: (0, 0)),
        ],
        out_specs=pl.BlockSpec((tm, D_MODEL), lambda i: (i, 0)),
        out_shape=jax.ShapeDtypeStruct((T, D_MODEL), F32),
        scratch_shapes=[pltpu.VMEM((tm, D_FF), BF16)],
        compiler_params=_params(("parallel",), 56),
    )(up, up, up, up, x1, conv_w, conv_b.reshape(1, D_FF), wd, norm_g.reshape(1, D_MODEL))


def _trunk(x, w):
    B, S, _ = x.shape
    T = B * S
    x2d = x.reshape(T, D_MODEL)
    proj = _norm_matmul(x2d, w["norm_mix_g"], w["w_in"], tm=1024, tn=640)
    proj3 = proj.reshape(B, S, PROJ_COLS)
    o_att = _attention(proj3, *_rope_tables(S))
    o_dn = _deltanet(proj3, w["conv_qkv_w"], w["dn_par"], w["out_norm_g"])
    x1 = _mix(x2d, o_att.reshape(T, ATT_GROUP_WIDTH), o_dn.reshape(T, DN_WIDTH), proj,
              w["w_branch_a"], w["w_branch_b"], w["w_out"], tm=512)
    up = _norm_matmul(x1, w["norm_ffn_g"], w["w_up"], tm=1024, tn=512)
    y = _ffn_out(up, x1, w["ffn_conv_w"], w["ffn_conv_b"], w["w_down"], w["norm_final_g"], tm=512, S=S)
    return y.reshape(B, S, D_MODEL)


def kernel(x_prompt, x_sample, norm_mix_g, w_in, conv_qkv_w, a_log_f, a_log_b, dt_bias_f, dt_bias_b, out_norm_g, w_branch_a, w_branch_b, w_out, norm_ffn_g, w_up, ffn_conv_w, ffn_conv_b, w_down, norm_final_g):
    att = 3 * ATT_WIDTH
    dn_end = att + 3 * DN_WIDTH
    z_end = dn_end + DN_WIDTH
    small_end = z_end + 4 * DN_HEADS
    w_in_r = jnp.concatenate([
        w_in[:, small_end:],
        w_in[:, dn_end:z_end],
        w_in[:, att:dn_end],
        w_in[:, :att],
        w_in[:, z_end:small_end],
        jnp.zeros((D_MODEL, LANES - 4 * DN_HEADS), F32),
    ], axis=1).astype(BF16)
    par = jnp.zeros((8, LANES), F32)
    par = par.at[0, 2 * DN_HEADS:4 * DN_HEADS].set(jnp.concatenate([a_log_f, a_log_b]))
    par = par.at[1, 2 * DN_HEADS:4 * DN_HEADS].set(jnp.concatenate([dt_bias_f, dt_bias_b]))
    w = dict(
        norm_mix_g=norm_mix_g, w_in=w_in_r, conv_qkv_w=conv_qkv_w, dn_par=par, out_norm_g=out_norm_g,
        w_branch_a=w_branch_a.astype(BF16), w_branch_b=w_branch_b.astype(BF16), w_out=w_out.astype(BF16),
        norm_ffn_g=norm_ffn_g, w_up=w_up.astype(BF16), ffn_conv_w=ffn_conv_w, ffn_conv_b=ffn_conv_b,
        w_down=w_down.astype(BF16), norm_final_g=norm_final_g,
    )
    return _trunk(x_prompt, w), _trunk(x_sample, w)
```

```python
import functools

import jax
import jax.numpy as jnp
from jax import lax
from jax.experimental import pallas as pl
from jax.experimental.pallas import tpu as pltpu

F32 = jnp.float32
BF16 = jnp.bfloat16
HIGHEST = lax.Precision.HIGHEST

D_MODEL = 1024
ATT_HEAD_DIM = 64
ATT_HEADS_PER_GROUP = 8
DILATIONS = (1, 4, 16)
ATT_HALF = 64
ATT_GROUP_WIDTH = ATT_HEADS_PER_GROUP * ATT_HEAD_DIM
ATT_WIDTH = len(DILATIONS) * ATT_GROUP_WIDTH
ROPE_DIM = ATT_HEAD_DIM // 4
ROPE_THETA = 500000.0
DN_HEADS = 8
DN_HEAD_DIM = 128
DN_WIDTH = DN_HEADS * DN_HEAD_DIM
SHORT_CONV = 5
DN_CHUNK = 64
D_FF = 2816
EPS = 1e-6
NEG_INF = -1e30

LANES = 128
MIB = 1024 * 1024

COL_GATE_A = 0
COL_GATE_B = 1024
COL_Z = 2048
COL_DN_Q = 3072
COL_DN_K = 4096
COL_DN_V = 5120
COL_ATT_Q = 6144
COL_ATT_K = COL_ATT_Q + ATT_WIDTH
COL_ATT_V = COL_ATT_K + ATT_WIDTH
COL_SMALL = COL_ATT_V + ATT_WIDTH
PROJ_COLS = COL_SMALL + LANES


def _params(semantics, vmem_mib):
    return pltpu.CompilerParams(dimension_semantics=semantics, vmem_limit_bytes=vmem_mib * MIB)


def _mm(a, b):
    return jnp.dot(a.astype(BF16), b.astype(BF16), preferred_element_type=F32)


def _mm_nt(a, b):
    return lax.dot_general(a.astype(BF16), b.astype(BF16), (((1,), (1,)), ((), ())),
                           preferred_element_type=F32)


def _norm_matmul_kernel(x_ref, g_ref, w_ref, o_ref, h_ref, *, row_chunk):
    @pl.when(pl.program_id(1) == 0)
    def _():
        def body(c, carry):
            r0 = pl.multiple_of(c * row_chunk, row_chunk)
            x = x_ref[pl.ds(r0, row_chunk), :]
            ms = jnp.mean(x * x, axis=-1, keepdims=True)
            h_ref[pl.ds(r0, row_chunk), :] = (x * lax.rsqrt(ms + EPS) * g_ref[...]).astype(BF16)
            return carry
        lax.fori_loop(0, x_ref.shape[0] // row_chunk, body, 0)

    o_ref[...] = jnp.dot(h_ref[...], w_ref[...], preferred_element_type=F32)


def _norm_matmul(x2d, gain, w_bf16, *, tm, tn):
    T, K = x2d.shape
    N = w_bf16.shape[1]
    return pl.pallas_call(
        functools.partial(_norm_matmul_kernel, row_chunk=128),
        grid=(T // tm, N // tn),
        in_specs=[
            pl.BlockSpec((tm, K), lambda i, j: (i, 0)),
            pl.BlockSpec((1, K), lambda i, j: (0, 0)),
            pl.BlockSpec((K, tn), lambda i, j: (0, j)),
        ],
        out_specs=pl.BlockSpec((tm, tn), lambda i, j: (i, j)),
        out_shape=jax.ShapeDtypeStruct((T, N), F32),
        scratch_shapes=[pltpu.VMEM((tm, K), BF16)],
        compiler_params=_params(("parallel", "arbitrary"), 48),
    )(x2d, gain.reshape(1, K), w_bf16)


ATT_QB = 128
ATT_UNROLL = 2


def _attn_group(q_ref, k_ref, v_ref, cos_ref, sa_ref, sb_ref, qs, ks, vs, acc_ref, m_ref, l_ref,
                *, S, dil, first):
    L = S // dil
    rc = min(L, 256)
    nlc = L // rc

    def rows_of(start, n):
        return pl.ds(start, n, stride=dil) if dil > 1 else pl.ds(start, n)

    def pre(idx, carry):
        r = idx // nlc
        lc = idx % nlc
        rows = rows_of(r + dil * lc * rc, rc)
        dst = pl.ds(pl.multiple_of(r * L + lc * rc, rc), rc)
        cs = cos_ref[rows, :]
        sa = sa_ref[rows, :]
        sb = sb_ref[rows, :]

        def rope(t):
            return t * cs + pltpu.roll(t, LANES - ROPE_DIM // 2, 1) * sa + pltpu.roll(t, ROPE_DIM // 2, 1) * sb

        q = rope(q_ref[0, rows, :]) * (ATT_HEAD_DIM ** -0.5)
        k = rope(k_ref[0, rows, :])
        v = v_ref[0, rows, :]
        for hh in range(2):
            sl = slice(hh * ATT_HEAD_DIM, (hh + 1) * ATT_HEAD_DIM)
            qs[hh, dst, :] = q[:, sl].astype(BF16)
            ks[hh, dst, :] = k[:, sl].astype(BF16)
            vs[hh, dst, :] = v[:, sl].astype(BF16)
        return carry

    lax.fori_loop(0, dil * nlc, pre, 0)

    nb = L // ATT_QB
    nk = min(2 * ATT_QB, L)
    col = lax.broadcasted_iota(jnp.int32, (ATT_QB, nk), 1)
    row = lax.broadcasted_iota(jnp.int32, (ATT_QB, nk), 0)

    def blk(it, carry):
        work = []
        for ub in range(ATT_UNROLL):
            idx = it * ATT_UNROLL + ub
            r = idx // nb
            n = idx % nb
            q0 = n * ATT_QB
            start = jnp.clip(q0 - ATT_HALF, 0, L - nk)
            valid = jnp.abs(col - row + (start - q0)) <= ATT_HALF
            rows = rows_of(r + dil * q0, ATT_QB)
            qsl = pl.ds(pl.multiple_of(r * L + q0, ATT_QB), ATT_QB)
            ksl = pl.ds(pl.multiple_of(r * L + start, ATT_HALF), nk)
            for hh in range(2):
                s = lax.dot_general(qs[hh, qsl, :], ks[hh, ksl, :], (((1,), (1,)), ((), ())),
                                    preferred_element_type=F32)
                work.append(dict(hh=hh, rows=rows, ksl=ksl, valid=valid, s=s))
        for w in work:
            s = jnp.where(w["valid"], w["s"], NEG_INF)
            w["m"] = jnp.max(s, axis=1, keepdims=True)
            p = jnp.exp(s - w["m"])
            w["l"] = jnp.sum(p, axis=1, keepdims=True)
            w["p"] = p.astype(BF16)
        for w in work:
            w["pv"] = jnp.dot(w["p"], vs[w["hh"], w["ksl"], :], preferred_element_type=F32)
        for w in work:
            hh, rows, m, l, pv = w["hh"], w["rows"], w["m"], w["l"], w["pv"]
            if first:
                m_ref[hh, rows, :] = m
                l_ref[hh, rows, :] = l
                acc_ref[hh, rows, :] = pv
            else:
                m_old = m_ref[hh, rows, :]
                m_new = jnp.maximum(m_old, m)
                a = jnp.exp(m_old - m_new)
                b = jnp.exp(m - m_new)
                m_ref[hh, rows, :] = m_new
                l_ref[hh, rows, :] = a * l_ref[hh, rows, :] + b * l
                acc_ref[hh, rows, :] = a * acc_ref[hh, rows, :] + b * pv
        return carry

    lax.fori_loop(0, dil * nb // ATT_UNROLL, blk, 0)


def _attn_kernel(q_ref, k_ref, v_ref, cos_ref, sa_ref, sb_ref, o_ref,
                 qs, ks, vs, acc_ref, m_ref, l_ref, *, S):
    g = pl.program_id(2)
    for gi, dil in enumerate(DILATIONS):
        @pl.when(g == gi)
        def _(gi=gi, dil=dil):
            _attn_group(q_ref, k_ref, v_ref, cos_ref, sa_ref, sb_ref, qs, ks, vs, acc_ref, m_ref, l_ref,
                        S=S, dil=dil, first=(gi == 0))

    @pl.when(g == len(DILATIONS) - 1)
    def _():
        rc = 256

        def fin(c, carry):
            sl = pl.ds(pl.multiple_of(c * rc, rc), rc)
            o = jnp.concatenate([acc_ref[hh, sl, :] / l_ref[hh, sl, :] for hh in range(2)], axis=1)
            o_ref[0, sl, :] = o.astype(o_ref.dtype)
            return carry

        lax.fori_loop(0, S // rc, fin, 0)


def _attention(proj3, cos_t, sa_t, sb_t):
    B, S, _ = proj3.shape
    ng = len(DILATIONS)

    def col_spec(col0):
        base = col0 // LANES
        per_group = ATT_GROUP_WIDTH // LANES
        return pl.BlockSpec((1, S, LANES), lambda b, j, g: (b, 0, base + g * per_group + j))

    tab_spec = pl.BlockSpec((S, LANES), lambda b, j, g: (0, 0))
    head_scr = pltpu.VMEM((2, S, ATT_HEAD_DIM), BF16)
    return pl.pallas_call(
        functools.partial(_attn_kernel, S=S),
        grid=(B, ATT_GROUP_WIDTH // LANES, ng),
        in_specs=[col_spec(COL_ATT_Q), col_spec(COL_ATT_K), col_spec(COL_ATT_V), tab_spec, tab_spec, tab_spec],
        out_specs=pl.BlockSpec((1, S, LANES), lambda b, j, g: (b, 0, j)),
        out_shape=jax.ShapeDtypeStruct((B, S, ATT_GROUP_WIDTH), BF16),
        scratch_shapes=[head_scr, head_scr, head_scr,
                        pltpu.VMEM((2, S, ATT_HEAD_DIM), F32),
                        pltpu.VMEM((2, S, 1), F32),
                        pltpu.VMEM((2, S, 1), F32)],
        compiler_params=_params(("parallel", "parallel", "arbitrary"), 48),
    )(proj3, proj3, proj3, cos_t, sa_t, sb_t)


def _rope_tables(S):
    half = ROPE_DIM // 2
    inv = ROPE_THETA ** (-jnp.arange(half, dtype=F32) / half)
    ang = jnp.arange(S, dtype=F32)[:, None] * inv[None, :]
    cos, sin = jnp.cos(ang), jnp.sin(ang)
    pad = ATT_HEAD_DIM - ROPE_DIM
    ones = jnp.ones((S, pad), F32)
    zeros = jnp.zeros((S, pad), F32)
    zh = jnp.zeros((S, half), F32)
    cos_h = jnp.concatenate([cos, cos, ones], axis=1)
    sa_h = jnp.concatenate([-sin, zh, zeros], axis=1)
    sb_h = jnp.concatenate([zh, sin, zeros], axis=1)
    tile = lambda t: jnp.concatenate([t, t], axis=1)
    return tile(cos_h), tile(sa_h), tile(sb_h)


DN_UNIT = 2 * DN_CHUNK
DN_UNROLL = 2
DN_KQ_ROWS = DN_HEAD_DIM + DN_CHUNK


def _dn_kernel(q_ref, k_ref, v_ref, z_ref, sm_ref, cwq_ref, cwk_ref, cwv_ref, par_ref, ng_ref, o_ref,
               xp, qn, kn, vn, kq_s, b_s, dec_s, oacc, *, S):
    h = pl.program_id(1)
    rc = 256
    halo = 8

    xp[0:halo, :] = jnp.zeros((halo, LANES), F32)
    xp[S + halo:S + 2 * halo, :] = jnp.zeros((halo, LANES), F32)
    for src, cw, dst, mode in ((q_ref, cwq_ref, qn, "q"), (k_ref, cwk_ref, kn, "k"), (v_ref, cwv_ref, vn, "v")):
        def cp(c, carry, src=src):
            r0 = pl.multiple_of(c * rc, rc)
            xp[pl.ds(r0 + halo, rc), :] = src[0, pl.ds(r0, rc), :]
            return carry

        lax.fori_loop(0, S // rc, cp, 0)

        def cv(c, carry, cw=cw, dst=dst, mode=mode):
            r0 = pl.multiple_of(c * rc, rc)
            win = xp[pl.ds(r0, rc + 2 * halo), :]
            off = halo - SHORT_CONV // 2
            y = win[off:off + rc] * cw[0:1, :]
            for i in range(1, SHORT_CONV):
                y = y + win[off + i:off + i + rc] * cw[i:i + 1, :]
            y = y * jax.nn.sigmoid(y)
            if mode != "v":
                y = y * lax.rsqrt(jnp.sum(y * y, axis=-1, keepdims=True) + EPS)
            if mode == "q":
                y = y * (DN_HEAD_DIM ** -0.5)
            dst[pl.ds(r0, rc), :] = y
            return carry

        lax.fori_loop(0, S // rc, cv, 0)

    lane = lax.broadcasted_iota(jnp.int32, (DN_UNIT, DN_UNIT), 1)
    rowi = lax.broadcasted_iota(jnp.int32, (DN_UNIT, DN_UNIT), 0)
    same = (rowi >= DN_CHUNK) == (lane >= DN_CHUNK)
    eye = jnp.where(lane == rowi, 1.0, 0.0)
    a_exp = jnp.exp(par_ref[0:1, :])
    dt_bias = par_ref[1:2, :]
    n_units = S // DN_UNIT

    def masks(d):
        if d == 0:
            return same & (lane <= rowi), same & (lane < rowi)
        return same & (lane >= rowi), same & (lane > rowi)

    tri3 = [jnp.concatenate([jnp.where(masks(d)[0], 1.0, 0.0).astype(BF16)] * 3, axis=1) for d in range(2)]

    def units(c, carry):
        chains = []
        for uu in range(DN_UNROLL):
            unit = c * DN_UNROLL + uu
            rows = pl.ds(pl.multiple_of(unit * DN_UNIT, DN_UNIT), DN_UNIT)
            sm = sm_ref[0, rows, :]
            beta_all = jax.nn.sigmoid(sm)
            g_all = -a_exp * jax.nn.softplus(sm + dt_bias)
            q = qn[rows, :]
            k = kn[rows, :]
            v = vn[rows, :]
            qk = _mm_nt(q, k)
            for d in range(2):
                beta = jnp.sum(jnp.where(lane == h + DN_HEADS * d, beta_all, 0.0), axis=1, keepdims=True)
                g = jnp.sum(jnp.where(lane == h + DN_HEADS * (2 + d), g_all, 0.0), axis=1, keepdims=True)
                gb = jnp.broadcast_to(g, (DN_UNIT, DN_UNIT))
                hi = gb.astype(BF16)
                r1 = gb - hi.astype(F32)
                mid = r1.astype(BF16)
                lo = (r1 - mid.astype(F32)).astype(BF16)
                chains.append(dict(unit=unit, rows=rows, d=d, q=q, k=k, qk=qk, kb=k * beta, vb=v * beta,
                                   g3=jnp.concatenate([hi, mid, lo], axis=0)))
        for ch in chains:
            d = ch["d"]
            tri, strict = masks(d)
            gc = jnp.dot(tri3[d], ch["g3"], preferred_element_type=F32)
            last0, last1 = (DN_CHUNK - 1, DN_UNIT - 1) if d == 0 else (0, DN_CHUNK)
            gl = jnp.where(rowi < DN_CHUNK, gc[last0:last0 + 1, :], gc[last1:last1 + 1, :])
            ch["dm"] = jnp.where(tri, jnp.exp(jnp.where(tri, gc - gc.T, 0.0)), 0.0)
            ch["gc"], ch["gl"] = gc, gl
        for ch in chains:
            strict = masks(ch["d"])[1]
            ch["nm"] = jnp.where(strict, -(_mm_nt(ch["kb"], ch["k"]) * ch["dm"]), 0.0)
        for ch in chains:
            ch["x"] = eye + ch["nm"]
            ch["p"] = _mm(ch["nm"], ch["nm"])
        for _ in range(4):
            for ch in chains:
                x_next = ch["x"] + _mm(ch["x"], ch["p"])
                ch["p"] = _mm(ch["p"], ch["p"])
                ch["x"] = x_next
        for ch in chains:
            eg = jnp.exp(ch["gc"])
            x = ch["x"] + _mm(ch["x"], ch["p"])
            ch["wu"] = _mm(x, jnp.concatenate([ch["kb"] * eg, ch["vb"]], axis=1))
            ch["qg"] = ch["q"] * eg
            ch["kg"] = ch["k"] * jnp.exp(ch["gl"] - ch["gc"])
        for ch in chains:
            iw = _mm(ch["qk"] * ch["dm"], ch["wu"])
            ch["qp"] = ch["qg"] - iw[:, :LANES]
            ch["oc"] = iw[:, LANES:]
        for ch in chains:
            d = ch["d"]
            dec = jnp.exp(ch["gl"])
            wu16 = ch["wu"].astype(BF16)
            for cc in range(2):
                ci = 2 * ch["unit"] + cc
                kg_c = jnp.where((rowi >= DN_CHUNK) == (cc == 1), ch["kg"], 0.0).astype(BF16)
                kb_ = lax.dot_general(kg_c, wu16, (((0,), (0,)), ((), ())), preferred_element_type=F32)
                base = pl.multiple_of(ci * DN_KQ_ROWS, DN_CHUNK)
                kq_s[d, pl.ds(base, DN_HEAD_DIM), :] = (-kb_[:, :LANES]).astype(BF16)
                kq_s[d, pl.ds(base + DN_HEAD_DIM, DN_CHUNK), :] = (
                    ch["qp"][cc * DN_CHUNK:(cc + 1) * DN_CHUNK].astype(BF16))
                b_s[d, pl.ds(pl.multiple_of(ci * DN_HEAD_DIM, DN_HEAD_DIM), DN_HEAD_DIM), :] = kb_[:, LANES:]
                dec_s[d, pl.ds(pl.multiple_of(ci * 8, 8), 8), :] = dec[cc * DN_CHUNK:cc * DN_CHUNK + 8]
        for uu in range(DN_UNROLL):
            oacc[chains[2 * uu]["rows"], :] = chains[2 * uu]["oc"] + chains[2 * uu + 1]["oc"]
        return carry

    lax.fori_loop(0, n_units // DN_UNROLL, units, 0)

    def step(d, ci, state):
        base = pl.multiple_of(ci * DN_KQ_ROWS, DN_CHUNK)
        r = jnp.dot(kq_s[d, pl.ds(base, DN_KQ_ROWS), :], state.astype(BF16), preferred_element_type=F32)
        oacc[pl.ds(pl.multiple_of(ci * DN_CHUNK, DN_CHUNK), DN_CHUNK), :] += r[DN_HEAD_DIM:]
        dec = dec_s[d, pl.ds(pl.multiple_of(ci * 8, 8), 1), :]
        b = b_s[d, pl.ds(pl.multiple_of(ci * DN_HEAD_DIM, DN_HEAD_DIM), DN_HEAD_DIM), :]
        return state * dec + r[:DN_HEAD_DIM] + b

    n_chunks = S // DN_CHUNK

    def rec(t, carry):
        sf, sb = carry
        cf = 2 * t
        cb = n_chunks - 1 - 2 * t
        sf = step(0, cf, sf)
        sb = step(1, cb, sb)
        sf = step(0, cf + 1, sf)
        sb = step(1, cb - 1, sb)
        return sf, sb

    zero_state = jnp.zeros((DN_HEAD_DIM, DN_HEAD_DIM), F32)
    lax.fori_loop(0, n_chunks // 2, rec, (zero_state, zero_state))

    def fin(c, carry):
        sl = pl.ds(pl.multiple_of(c * rc, rc), rc)
        o = oacc[sl, :]
        o = o * lax.rsqrt(jnp.mean(o * o, axis=-1, keepdims=True) + EPS) * ng_ref[...]
        z = z_ref[0, sl, :]
        o_ref[0, sl, :] = (o * (z * jax.nn.sigmoid(z))).astype(o_ref.dtype)
        return carry

    lax.fori_loop(0, S // rc, fin, 0)


def _deltanet(proj3, conv_w, par, norm_g):
    B, S, _ = proj3.shape

    def col_spec(col0):
        base = col0 // LANES
        return pl.BlockSpec((1, S, LANES), lambda b, h: (b, 0, base + h))

    def conv_spec(part):
        return pl.BlockSpec((SHORT_CONV, LANES), lambda b, h: (0, part * DN_HEADS + h))

    seq_f32 = pltpu.VMEM((S, LANES), F32)
    n_chunks = S // DN_CHUNK
    return pl.pallas_call(
        functools.partial(_dn_kernel, S=S),
        grid=(B, DN_HEADS),
        in_specs=[col_spec(COL_DN_Q), col_spec(COL_DN_K), col_spec(COL_DN_V), col_spec(COL_Z),
                  pl.BlockSpec((1, S, LANES), lambda b, h: (b, 0, COL_SMALL // LANES)),
                  conv_spec(0), conv_spec(1), conv_spec(2),
                  pl.BlockSpec((8, LANES), lambda b, h: (0, 0)),
                  pl.BlockSpec((1, LANES), lambda b, h: (0, 0))],
        out_specs=pl.BlockSpec((1, S, LANES), lambda b, h: (b, 0, h)),
        out_shape=jax.ShapeDtypeStruct((B, S, DN_WIDTH), BF16),
        scratch_shapes=[pltpu.VMEM((S + 16, LANES), F32), seq_f32, seq_f32, seq_f32,
                        pltpu.VMEM((2, n_chunks * DN_KQ_ROWS, LANES), BF16),
                        pltpu.VMEM((2, n_chunks * DN_HEAD_DIM, LANES), F32),
                        pltpu.VMEM((2, n_chunks * 8, LANES), F32),
                        seq_f32],
        compiler_params=_params(("parallel", "parallel"), 56),
    )(proj3, proj3, proj3, proj3, proj3, conv_w, conv_w, conv_w, par, norm_g.reshape(1, LANES))


def _mix_kernel(x_ref, oa_ref, ob_ref, ga_ref, gb_ref, wa_ref, wb_ref, wo_ref, o_ref):
    ya = jnp.dot(oa_ref[...], wa_ref[...], preferred_element_type=F32)
    yb = jnp.dot(ob_ref[...], wb_ref[...], preferred_element_type=F32)
    mix = jax.nn.sigmoid(ga_ref[...]) * ya + jax.nn.sigmoid(gb_ref[...]) * yb
    o_ref[...] = x_ref[...] + jnp.dot(mix.astype(BF16), wo_ref[...], preferred_element_type=F32)


def _mix(x2d, o_att, o_dn, proj, wa, wb, wo, *, tm):
    T = x2d.shape[0]
    row = lambda width, cb=0: pl.BlockSpec((tm, width), lambda i: (i, cb))
    full = lambda a: pl.BlockSpec(a.shape, lambda i: (0, 0))
    return pl.pallas_call(
        _mix_kernel,
        grid=(T // tm,),
        in_specs=[row(D_MODEL), row(ATT_GROUP_WIDTH), row(DN_WIDTH),
                  row(D_MODEL, COL_GATE_A // D_MODEL), row(D_MODEL, COL_GATE_B // D_MODEL),
                  full(wa), full(wb), full(wo)],
        out_specs=row(D_MODEL),
        out_shape=jax.ShapeDtypeStruct((T, D_MODEL), F32),
        compiler_params=_params(("parallel",), 48),
    )(x2d, o_att, o_dn, proj, proj, wa, wb, wo)


FFN_KC = 256


def _ffn_out_kernel(g_ref, v_ref, gp_ref, gn_ref, x_ref, cw_ref, cb_ref, wd_ref, nf_ref, o_ref, act_ref,
                    *, tiles_per_seq):
    i = pl.program_id(0)
    tm = g_ref.shape[0]
    pos = i % tiles_per_seq
    keep_prev = jnp.where(pos == 0, 0.0, 1.0)
    keep_next = jnp.where(pos == tiles_per_seq - 1, 0.0, 1.0)
    row = lax.broadcasted_iota(jnp.int32, (tm, FFN_KC), 0)
    for c in range(D_FF // FFN_KC):
        sl = slice(c * FFN_KC, (c + 1) * FFN_KC)
        g = g_ref[:, sl]
        prev_row = gp_ref[7:8, sl] * keep_prev
        next_row = gn_ref[0:1, sl] * keep_next
        g_prev = jnp.where(row == 0, prev_row, pltpu.roll(g, 1, 0))
        g_next = jnp.where(row == tm - 1, next_row, pltpu.roll(g, tm - 1, 0))
        conv = g_prev * cw_ref[0:1, sl] + g * cw_ref[1:2, sl] + g_next * cw_ref[2:3, sl] + cb_ref[:, sl]
        gelu = 0.5 * conv * (1.0 + lax.erf(conv * (2.0 ** -0.5)))
        act_ref[:, sl] = (gelu * v_ref[:, sl]).astype(BF16)
    x2 = x_ref[...] + jnp.dot(act_ref[...], wd_ref[...], preferred_element_type=F32)
    ms = jnp.mean(x2 * x2, axis=-1, keepdims=True)
    o_ref[...] = x2 * lax.rsqrt(ms + EPS) * nf_ref[...]


def _ffn_out(up, x1, conv_w, conv_b, wd, norm_g, *, tm, S):
    T = x1.shape[0]
    tiles_per_seq = S // tm
    hb = tm // 8
    last_hb = T // 8 - 1
    return pl.pallas_call(
        functools.partial(_ffn_out_kernel, tiles_per_seq=tiles_per_seq),
        grid=(T // tm,),
        in_specs=[
            pl.BlockSpec((tm, D_FF), lambda i: (i, 0)),
            pl.BlockSpec((tm, D_FF), lambda i: (i, 1)),
            pl.BlockSpec((8, D_FF), lambda i: (jnp.maximum(i * hb - 1, 0), 0)),
            pl.BlockSpec((8, D_FF), lambda i: (jnp.minimum((i + 1) * hb, last_hb), 0)),
            pl.BlockSpec((tm, D_MODEL), lambda i: (i, 0)),
            pl.BlockSpec((3, D_FF), lambda i: (0, 0)),
            pl.BlockSpec((1, D_FF), lambda i: (0, 0)),
            pl.BlockSpec((D_FF, D_MODEL), lambda i: (0, 0)),
            pl.BlockSpec((1, D_MODEL), lambda i: (0, 0)),
        ],
        out_specs=pl.BlockSpec((tm, D_MODEL), lambda i: (i, 0)),
        out_shape=jax.ShapeDtypeStruct((T, D_MODEL), F32),
        scratch_shapes=[pltpu.VMEM((tm, D_FF), BF16)],
        compiler_params=_params(("parallel",), 56),
    )(up, up, up, up, x1, conv_w, conv_b.reshape(1, D_FF), wd, norm_g.reshape(1, D_MODEL))


def _trunk(x, w):
    B, S, _ = x.shape
    T = B * S
    x2d = x.reshape(T, D_MODEL)
    proj = _norm_matmul(x2d, w["norm_mix_g"], w["w_in"], tm=1024, tn=640)
    proj3 = proj.reshape(B, S, PROJ_COLS)
    o_att = _attention(proj3, *_rope_tables(S))
    o_dn = _deltanet(proj3, w["conv_qkv_w"], w["dn_par"], w["out_norm_g"])
    x1 = _mix(x2d, o_att.reshape(T, ATT_GROUP_WIDTH), o_dn.reshape(T, DN_WIDTH), proj,
              w["w_branch_a"], w["w_branch_b"], w["w_out"], tm=512)
    up = _norm_matmul(x1, w["norm_ffn_g"], w["w_up"], tm=1024, tn=512)
    y = _ffn_out(up, x1, w["ffn_conv_w"], w["ffn_conv_b"], w["w_down"], w["norm_final_g"], tm=512, S=S)
    return y.reshape(B, S, D_MODEL)


def kernel(x_prompt, x_sample, norm_mix_g, w_in, conv_qkv_w, a_log_f, a_log_b, dt_bias_f, dt_bias_b, out_norm_g, w_branch_a, w_branch_b, w_out, norm_ffn_g, w_up, ffn_conv_w, ffn_conv_b, w_down, norm_final_g):
    att = 3 * ATT_WIDTH
    dn_end = att + 3 * DN_WIDTH
    z_end = dn_end + DN_WIDTH
    small_end = z_end + 4 * DN_HEADS
    w_in_r = jnp.concatenate([
        w_in[:, small_end:],
        w_in[:, dn_end:z_end],
        w_in[:, att:dn_end],
        w_in[:, :att],
        w_in[:, z_end:small_end],
        jnp.zeros((D_MODEL, LANES - 4 * DN_HEADS), F32),
    ], axis=1).astype(BF16)
    par = jnp.zeros((8, LANES), F32)
    par = par.at[0, 2 * DN_HEADS:4 * DN_HEADS].set(jnp.concatenate([a_log_f, a_log_b]))
    par = par.at[1, 2 * DN_HEADS:4 * DN_HEADS].set(jnp.concatenate([dt_bias_f, dt_bias_b]))
    w = dict(
        norm_mix_g=norm_mix_g, w_in=w_in_r, conv_qkv_w=conv_qkv_w, dn_par=par, out_norm_g=out_norm_g,
        w_branch_a=w_branch_a.astype(BF16), w_branch_b=w_branch_b.astype(BF16), w_out=w_out.astype(BF16),
        norm_ffn_g=norm_ffn_g, w_up=w_up.astype(BF16), ffn_conv_w=ffn_conv_w, ffn_conv_b=ffn_conv_b,
        w_down=w_down.astype(BF16), norm_final_g=norm_final_g,
    )
    return _trunk(x_prompt, w), _trunk(x_sample, w)
```

```python
import functools

import jax
import jax.numpy as jnp
from jax import lax
from jax.experimental import pallas as pl
from jax.experimental.pallas import tpu as pltpu

F32 = jnp.float32
BF16 = jnp.bfloat16

D_MODEL = 1024
ATT_HEAD_DIM = 64
ATT_HEADS_PER_GROUP = 8
DILATIONS = (1, 4, 16)
ATT_HALF = 64
ATT_GROUP_WIDTH = ATT_HEADS_PER_GROUP * ATT_HEAD_DIM
ATT_WIDTH = len(DILATIONS) * ATT_GROUP_WIDTH
ROPE_DIM = ATT_HEAD_DIM // 4
ROPE_THETA = 500000.0
DN_HEADS = 8
DN_HEAD_DIM = 128
DN_WIDTH = DN_HEADS * DN_HEAD_DIM
SHORT_CONV = 5
D_FF = 2816
EPS = 1e-6
NEG_INF = -1e30

LANES = 128
MIB = 1024 * 1024

COL_GATE_A = 0
COL_GATE_B = 1024
COL_Z = 2048
COL_DN_Q = 3072
COL_DN_K = 4096
COL_DN_V = 5120
COL_ATT_Q = 6144
COL_ATT_K = COL_ATT_Q + ATT_WIDTH
COL_ATT_V = COL_ATT_K + ATT_WIDTH
COL_SMALL = COL_ATT_V + ATT_WIDTH
PROJ_COLS = COL_SMALL + LANES


def _params(semantics, vmem_mib):
    return pltpu.CompilerParams(dimension_semantics=semantics, vmem_limit_bytes=vmem_mib * MIB)


def _mm(a, b):
    return jnp.dot(a.astype(BF16), b.astype(BF16), preferred_element_type=F32)


def _mm_nt(a, b):
    return lax.dot_general(a.astype(BF16), b.astype(BF16), (((1,), (1,)), ((), ())),
                           preferred_element_type=F32)


def _mm_tn(a, b):
    return lax.dot_general(a.astype(BF16), b.astype(BF16), (((0,), (0,)), ((), ())),
                           preferred_element_type=F32)


def _norm_matmul_kernel(x_ref, g_ref, w_ref, o_ref, h_ref, *, row_chunk):
    @pl.when(pl.program_id(1) == 0)
    def _():
        def body(c, carry):
            r0 = pl.multiple_of(c * row_chunk, row_chunk)
            x = x_ref[pl.ds(r0, row_chunk), :]
            ms = jnp.mean(x * x, axis=-1, keepdims=True)
            h_ref[pl.ds(r0, row_chunk), :] = (x * lax.rsqrt(ms + EPS) * g_ref[...]).astype(BF16)
            return carry
        lax.fori_loop(0, x_ref.shape[0] // row_chunk, body, 0)

    o_ref[...] = jnp.dot(h_ref[...], w_ref[...], preferred_element_type=F32)


def _norm_matmul(x2d, gain, w_bf16, *, tm, tn):
    T, K = x2d.shape
    N = w_bf16.shape[1]
    return pl.pallas_call(
        functools.partial(_norm_matmul_kernel, row_chunk=128),
        grid=(T // tm, N // tn),
        in_specs=[
            pl.BlockSpec((tm, K), lambda i, j: (i, 0)),
            pl.BlockSpec((1, K), lambda i, j: (0, 0)),
            pl.BlockSpec((K, tn), lambda i, j: (0, j)),
        ],
        out_specs=pl.BlockSpec((tm, tn), lambda i, j: (i, j)),
        out_shape=jax.ShapeDtypeStruct((T, N), F32),
        scratch_shapes=[pltpu.VMEM((tm, K), BF16)],
        compiler_params=_params(("parallel", "arbitrary"), 48),
    )(x2d, gain.reshape(1, K), w_bf16)


ATT_QB = 128
ATT_UNROLL = 2


def _attn_group(q_ref, k_ref, v_ref, cos_ref, sa_ref, sb_ref, qs, ks, vs, acc_ref, m_ref, l_ref,
                *, S, dil, first):
    L = S // dil
    rc = min(L, 256)
    nlc = L // rc

    def rows_of(start, n):
        return pl.ds(start, n, stride=dil) if dil > 1 else pl.ds(start, n)

    def pre(idx, carry):
        r = idx // nlc
        lc = idx % nlc
        rows = rows_of(r + dil * lc * rc, rc)
        pos = r * L + lc * rc
        cs = cos_ref[rows, :]
        sa = sa_ref[rows, :]
        sb = sb_ref[rows, :]

        def rope(t):
            return t * cs + pltpu.roll(t, LANES - ROPE_DIM // 2, 1) * sa + pltpu.roll(t, ROPE_DIM // 2, 1) * sb

        q = rope(q_ref[0, rows, :]) * (ATT_HEAD_DIM ** -0.5)
        ks[pl.ds(pl.multiple_of(pos, rc), rc), :] = rope(k_ref[0, rows, :]).astype(BF16)
        vs[pl.ds(pl.multiple_of(pos, rc), rc), :] = v_ref[0, rows, :].astype(BF16)
        head_a = lax.broadcasted_iota(jnp.int32, (rc, LANES), 1) < ATT_HEAD_DIM
        qa = jnp.where(head_a, q, 0.0).astype(BF16)
        qb = jnp.where(head_a, 0.0, q).astype(BF16)
        for sub in range(rc // ATT_QB):
            dst = pl.multiple_of(2 * pos + 2 * sub * ATT_QB, 2 * ATT_QB)
            qs[pl.ds(dst, ATT_QB), :] = qa[sub * ATT_QB:(sub + 1) * ATT_QB]
            qs[pl.ds(dst + ATT_QB, ATT_QB), :] = qb[sub * ATT_QB:(sub + 1) * ATT_QB]
        return carry

    lax.fori_loop(0, dil * nlc, pre, 0)

    nb = L // ATT_QB
    nk = min(2 * ATT_QB, L)
    col = lax.broadcasted_iota(jnp.int32, (ATT_QB, nk), 1)
    row = lax.broadcasted_iota(jnp.int32, (ATT_QB, nk), 0)
    head_a = lax.broadcasted_iota(jnp.int32, (ATT_QB, LANES), 1) < ATT_HEAD_DIM

    def both(t):
        return jnp.where(head_a, t[:ATT_QB], t[ATT_QB:])

    def blk(it, carry):
        work = []
        for ub in range(ATT_UNROLL):
            idx = it * ATT_UNROLL + ub
            r = idx // nb
            n = idx % nb
            q0 = n * ATT_QB
            start = jnp.clip(q0 - ATT_HALF, 0, L - nk)
            valid = jnp.abs(col - row + (start - q0)) <= ATT_HALF
            qsl = pl.ds(pl.multiple_of(2 * (r * L + q0), 2 * ATT_QB), 2 * ATT_QB)
            ksl = pl.ds(pl.multiple_of(r * L + start, ATT_HALF), nk)
            s = lax.dot_general(qs[qsl, :], ks[ksl, :], (((1,), (1,)), ((), ())), preferred_element_type=F32)
            work.append(dict(rows=rows_of(r + dil * q0, ATT_QB), ksl=ksl, s=s,
                             valid=jnp.concatenate([valid, valid], axis=0)))
        for w in work:
            s = jnp.where(w["valid"], w["s"], NEG_INF)
            w["m"] = jnp.max(s, axis=1, keepdims=True)
            p = jnp.exp(s - w["m"])
            w["l"] = jnp.sum(p, axis=1, keepdims=True)
            w["p"] = p.astype(BF16)
        for w in work:
            w["pv"] = jnp.dot(w["p"], vs[w["ksl"], :], preferred_element_type=F32)
        for w in work:
            rows, m, l, pv = w["rows"], both(w["m"]), both(w["l"]), both(w["pv"])
            if first:
                m_ref[rows, :] = m
                l_ref[rows, :] = l
                acc_ref[rows, :] = pv
            else:
                m_old = m_ref[rows, :]
                m_new = jnp.maximum(m_old, m)
                a = jnp.exp(m_old - m_new)
                b = jnp.exp(m - m_new)
                m_ref[rows, :] = m_new
                l_ref[rows, :] = a * l_ref[rows, :] + b * l
                acc_ref[rows, :] = a * acc_ref[rows, :] + b * pv
        return carry

    lax.fori_loop(0, dil * nb // ATT_UNROLL, blk, 0)


def _attn_kernel(q_ref, k_ref, v_ref, cos_ref, sa_ref, sb_ref, o_ref,
                 qs, ks, vs, acc_ref, m_ref, l_ref, *, S):
    g = pl.program_id(2)
    for gi, dil in enumerate(DILATIONS):
        @pl.when(g == gi)
        def _(gi=gi, dil=dil):
            _attn_group(q_ref, k_ref, v_ref, cos_ref, sa_ref, sb_ref, qs, ks, vs, acc_ref, m_ref, l_ref,
                        S=S, dil=dil, first=(gi == 0))

    @pl.when(g == len(DILATIONS) - 1)
    def _():
        rc = 256

        def fin(c, carry):
            sl = pl.ds(pl.multiple_of(c * rc, rc), rc)
            o_ref[0, sl, :] = (acc_ref[sl, :] / l_ref[sl, :]).astype(o_ref.dtype)
            return carry

        lax.fori_loop(0, S // rc, fin, 0)


def _attention(proj3, cos_t, sa_t, sb_t):
    B, S, _ = proj3.shape
    ng = len(DILATIONS)

    def col_spec(col0):
        base = col0 // LANES
        per_group = ATT_GROUP_WIDTH // LANES
        return pl.BlockSpec((1, S, LANES), lambda b, j, g: (b, 0, base + g * per_group + j))

    tab_spec = pl.BlockSpec((S, LANES), lambda b, j, g: (0, 0))
    seq_bf16 = pltpu.VMEM((S, LANES), BF16)
    seq_f32 = pltpu.VMEM((S, LANES), F32)
    return pl.pallas_call(
        functools.partial(_attn_kernel, S=S),
        grid=(B, ATT_GROUP_WIDTH // LANES, ng),
        in_specs=[col_spec(COL_ATT_Q), col_spec(COL_ATT_K), col_spec(COL_ATT_V), tab_spec, tab_spec, tab_spec],
        out_specs=pl.BlockSpec((1, S, LANES), lambda b, j, g: (b, 0, j)),
        out_shape=jax.ShapeDtypeStruct((B, S, ATT_GROUP_WIDTH), BF16),
        scratch_shapes=[pltpu.VMEM((2 * S, LANES), BF16), seq_bf16, seq_bf16, seq_f32, seq_f32, seq_f32],
        compiler_params=_params(("parallel", "parallel", "arbitrary"), 48),
    )(proj3, proj3, proj3, cos_t, sa_t, sb_t)


def _rope_tables(S):
    half = ROPE_DIM // 2
    inv = ROPE_THETA ** (-jnp.arange(half, dtype=F32) / half)
    ang = jnp.arange(S, dtype=F32)[:, None] * inv[None, :]
    cos, sin = jnp.cos(ang), jnp.sin(ang)
    pad = ATT_HEAD_DIM - ROPE_DIM
    ones = jnp.ones((S, pad), F32)
    zeros = jnp.zeros((S, pad), F32)
    zh = jnp.zeros((S, half), F32)
    cos_h = jnp.concatenate([cos, cos, ones], axis=1)
    sa_h = jnp.concatenate([-sin, zh, zeros], axis=1)
    sb_h = jnp.concatenate([zh, sin, zeros], axis=1)
    tile = lambda t: jnp.concatenate([t, t], axis=1)
    return tile(cos_h), tile(sa_h), tile(sb_h)


DN_BLK = 128
DN_UNROLL = 4
DN_DOUBLINGS = 5


def _dn_kernel(q_ref, k_ref, v_ref, z_ref, sm_ref, cwq_ref, cwk_ref, cwv_ref, par_ref, ng_ref, o_ref,
               xp, qn, kn, vn, kq_s, b_s, dec_s, oacc, *, S):
    h = pl.program_id(1)
    rc = 256
    halo = 8

    xp[0:halo, :] = jnp.zeros((halo, LANES), F32)
    xp[S + halo:S + 2 * halo, :] = jnp.zeros((halo, LANES), F32)
    for src, cw, dst, mode in ((q_ref, cwq_ref, qn, "q"), (k_ref, cwk_ref, kn, "k"), (v_ref, cwv_ref, vn, "v")):
        def cp(c, carry, src=src):
            r0 = pl.multiple_of(c * rc, rc)
            xp[pl.ds(r0 + halo, rc), :] = src[0, pl.ds(r0, rc), :]
            return carry

        lax.fori_loop(0, S // rc, cp, 0)

        def cv(c, carry, cw=cw, dst=dst, mode=mode):
            r0 = pl.multiple_of(c * rc, rc)
            win = xp[pl.ds(r0, rc + 2 * halo), :]
            off = halo - SHORT_CONV // 2
            y = win[off:off + rc] * cw[0:1, :]
            for i in range(1, SHORT_CONV):
                y = y + win[off + i:off + i + rc] * cw[i:i + 1, :]
            y = y * jax.nn.sigmoid(y)
            if mode != "v":
                y = y * lax.rsqrt(jnp.sum(y * y, axis=-1, keepdims=True) + EPS)
            if mode == "q":
                y = y * (DN_HEAD_DIM ** -0.5)
            dst[pl.ds(r0, rc), :] = y
            return carry

        lax.fori_loop(0, S // rc, cv, 0)

    lane = lax.broadcasted_iota(jnp.int32, (DN_BLK, DN_BLK), 1)
    rowi = lax.broadcasted_iota(jnp.int32, (DN_BLK, DN_BLK), 0)
    eye = jnp.where(lane == rowi, 1.0, 0.0)
    same_half = (rowi >= DN_BLK // 2) == (lane >= DN_BLK // 2)
    a_exp = jnp.exp(par_ref[0:1, :])
    dt_bias = par_ref[1:2, :]
    n_blocks = S // DN_BLK

    def masks(d):
        return (lane <= rowi, lane < rowi) if d == 0 else (lane >= rowi, lane > rowi)

    tri3 = [jnp.concatenate([jnp.where(masks(d)[0], 1.0, 0.0).astype(BF16)] * 3, axis=1) for d in range(2)]

    def prepass(c, carry):
        chains = []
        for uu in range(DN_UNROLL):
            blk = c * DN_UNROLL + uu
            rows = pl.ds(pl.multiple_of(blk * DN_BLK, DN_BLK), DN_BLK)
            sm = sm_ref[0, rows, :]
            beta_all = jax.nn.sigmoid(sm)
            g_all = -a_exp * jax.nn.softplus(sm + dt_bias)
            q = qn[rows, :]
            k = kn[rows, :]
            v = vn[rows, :]
            qk = _mm_nt(q, k)
            for d in range(2):
                beta = jnp.sum(jnp.where(lane == h + DN_HEADS * d, beta_all, 0.0), axis=1, keepdims=True)
                g = jnp.sum(jnp.where(lane == h + DN_HEADS * (2 + d), g_all, 0.0), axis=1, keepdims=True)
                gb = jnp.broadcast_to(g, (DN_BLK, DN_BLK))
                hi = gb.astype(BF16)
                r1 = gb - hi.astype(F32)
                mid = r1.astype(BF16)
                lo = (r1 - mid.astype(F32)).astype(BF16)
                chains.append(dict(blk=blk, rows=rows, d=d, q=q, k=k, qk=qk, kb=k * beta, vb=v * beta,
                                   g3=jnp.concatenate([hi, mid, lo], axis=0)))
        for ch in chains:
            d = ch["d"]
            tri = masks(d)[0]
            gc = jnp.dot(tri3[d], ch["g3"], preferred_element_type=F32)
            last = DN_BLK - 1 if d == 0 else 0
            ch["gc"], ch["gl"] = gc, gc[last:last + 1, :]
            ch["dm"] = jnp.where(tri, jnp.exp(jnp.where(tri, gc - gc.T, 0.0)), 0.0)
        for ch in chains:
            nm = jnp.where(masks(ch["d"])[1], -(_mm_nt(ch["kb"], ch["k"]) * ch["dm"]), 0.0)
            ch["nd"] = jnp.where(same_half, nm, 0.0)
            ch["no"] = jnp.where(same_half, 0.0, nm)
        for ch in chains:
            ch["x"] = eye + ch["nd"]
            ch["p"] = _mm(ch["nd"], ch["nd"])
        for _ in range(DN_DOUBLINGS - 1):
            for ch in chains:
                x_next = ch["x"] + _mm(ch["x"], ch["p"])
                ch["p"] = _mm(ch["p"], ch["p"])
                ch["x"] = x_next
        for ch in chains:
            ch["x"] = ch["x"] + _mm(ch["x"], ch["p"])
            ch["y"] = _mm(ch["no"], ch["x"])
        for ch in chains:
            eg = jnp.exp(ch["gc"])
            x = ch["x"] + _mm(ch["x"], ch["y"])
            ch["wu"] = _mm(x, jnp.concatenate([ch["kb"] * eg, ch["vb"]], axis=1))
            ch["qg"] = ch["q"] * eg
            ch["kg"] = ch["k"] * jnp.exp(ch["gl"] - ch["gc"])
        for ch in chains:
            iw = _mm(ch["qk"] * ch["dm"], ch["wu"])
            ch["qp"] = ch["qg"] - iw[:, :LANES]
            ch["oc"] = iw[:, LANES:]
        for ch in chains:
            d, blk = ch["d"], ch["blk"]
            kb_ = _mm_tn(ch["kg"], ch["wu"])
            base = pl.multiple_of(blk * 2 * DN_BLK, 2 * DN_BLK)
            kq_s[d, pl.ds(base, DN_BLK), :] = (-kb_[:, :LANES]).astype(BF16)
            kq_s[d, pl.ds(base + DN_BLK, DN_BLK), :] = ch["qp"].astype(BF16)
            b_s[d, ch["rows"], :] = kb_[:, LANES:]
            dec_s[d, pl.ds(pl.multiple_of(blk * 8, 8), 8), :] = jnp.broadcast_to(jnp.exp(ch["gl"]), (8, LANES))
        for uu in range(DN_UNROLL):
            oacc[chains[2 * uu]["rows"], :] = chains[2 * uu]["oc"] + chains[2 * uu + 1]["oc"]
        return carry

    lax.fori_loop(0, n_blocks // DN_UNROLL, prepass, 0)

    def step(d, blk, state):
        base = pl.multiple_of(blk * 2 * DN_BLK, 2 * DN_BLK)
        r = jnp.dot(kq_s[d, pl.ds(base, 2 * DN_BLK), :], state.astype(BF16), preferred_element_type=F32)
        rows = pl.ds(pl.multiple_of(blk * DN_BLK, DN_BLK), DN_BLK)
        oacc[rows, :] += r[DN_BLK:]
        dec = dec_s[d, pl.ds(pl.multiple_of(blk * 8, 8), 1), :]
        return state * dec + r[:DN_BLK] + b_s[d, rows, :]

    def rec(t, carry):
        sf, sb = carry
        bf = 2 * t
        bb = n_blocks - 1 - 2 * t
        sf = step(0, bf, sf)
        sb = step(1, bb, sb)
        sf = step(0, bf + 1, sf)
        sb = step(1, bb - 1, sb)
        return sf, sb

    zero_state = jnp.zeros((DN_HEAD_DIM, DN_HEAD_DIM), F32)
    lax.fori_loop(0, n_blocks // 2, rec, (zero_state, zero_state))

    def fin(c, carry):
        sl = pl.ds(pl.multiple_of(c * rc, rc), rc)
        o = oacc[sl, :]
        o = o * lax.rsqrt(jnp.mean(o * o, axis=-1, keepdims=True) + EPS) * ng_ref[...]
        z = z_ref[0, sl, :]
        o_ref[0, sl, :] = (o * (z * jax.nn.sigmoid(z))).astype(o_ref.dtype)
        return carry

    lax.fori_loop(0, S // rc, fin, 0)


def _deltanet(proj3, conv_w, par, norm_g):
    B, S, _ = proj3.shape

    def col_spec(col0):
        base = col0 // LANES
        return pl.BlockSpec((1, S, LANES), lambda b, h: (b, 0, base + h))

    def conv_spec(part):
        return pl.BlockSpec((SHORT_CONV, LANES), lambda b, h: (0, part * DN_HEADS + h))

    seq_f32 = pltpu.VMEM((S, LANES), F32)
    return pl.pallas_call(
        functools.partial(_dn_kernel, S=S),
        grid=(B, DN_HEADS),
        in_specs=[col_spec(COL_DN_Q), col_spec(COL_DN_K), col_spec(COL_DN_V), col_spec(COL_Z),
                  pl.BlockSpec((1, S, LANES), lambda b, h: (b, 0, COL_SMALL // LANES)),
                  conv_spec(0), conv_spec(1), conv_spec(2),
                  pl.BlockSpec((8, LANES), lambda b, h: (0, 0)),
                  pl.BlockSpec((1, LANES), lambda b, h: (0, 0))],
        out_specs=pl.BlockSpec((1, S, LANES), lambda b, h: (b, 0, h)),
        out_shape=jax.ShapeDtypeStruct((B, S, DN_WIDTH), BF16),
        scratch_shapes=[pltpu.VMEM((S + 16, LANES), F32), seq_f32, seq_f32, seq_f32,
                        pltpu.VMEM((2, 2 * S, LANES), BF16),
                        pltpu.VMEM((2, S, LANES), F32),
                        pltpu.VMEM((2, S // DN_BLK * 8, LANES), F32),
                        seq_f32],
        compiler_params=_params(("parallel", "parallel"), 56),
    )(proj3, proj3, proj3, proj3, proj3, conv_w, conv_w, conv_w, par, norm_g.reshape(1, LANES))


def _mix_kernel(x_ref, oa_ref, ob_ref, ga_ref, gb_ref, wa_ref, wb_ref, wo_ref, o_ref):
    ya = jnp.dot(oa_ref[...], wa_ref[...], preferred_element_type=F32)
    yb = jnp.dot(ob_ref[...], wb_ref[...], preferred_element_type=F32)
    mix = jax.nn.sigmoid(ga_ref[...]) * ya + jax.nn.sigmoid(gb_ref[...]) * yb
    o_ref[...] = x_ref[...] + jnp.dot(mix.astype(BF16), wo_ref[...], preferred_element_type=F32)


def _mix(x2d, o_att, o_dn, proj, wa, wb, wo, *, tm):
    T = x2d.shape[0]
    row = lambda width, cb=0: pl.BlockSpec((tm, width), lambda i: (i, cb))
    full = lambda a: pl.BlockSpec(a.shape, lambda i: (0, 0))
    return pl.pallas_call(
        _mix_kernel,
        grid=(T // tm,),
        in_specs=[row(D_MODEL), row(ATT_GROUP_WIDTH), row(DN_WIDTH),
                  row(D_MODEL, COL_GATE_A // D_MODEL), row(D_MODEL, COL_GATE_B // D_MODEL),
                  full(wa), full(wb), full(wo)],
        out_specs=row(D_MODEL),
        out_shape=jax.ShapeDtypeStruct((T, D_MODEL), F32),
        compiler_params=_params(("parallel",), 48),
    )(x2d, o_att, o_dn, proj, proj, wa, wb, wo)


FFN_KC = 256


def _ffn_out_kernel(g_ref, v_ref, gp_ref, gn_ref, x_ref, cw_ref, cb_ref, wd_ref, nf_ref, o_ref, act_ref,
                    *, tiles_per_seq):
    i = pl.program_id(0)
    tm = g_ref.shape[0]
    pos = i % tiles_per_seq
    keep_prev = jnp.where(pos == 0, 0.0, 1.0)
    keep_next = jnp.where(pos == tiles_per_seq - 1, 0.0, 1.0)
    row = lax.broadcasted_iota(jnp.int32, (tm, FFN_KC), 0)
    for c in range(D_FF // FFN_KC):
        sl = slice(c * FFN_KC, (c + 1) * FFN_KC)
        g = g_ref[:, sl]
        prev_row = gp_ref[7:8, sl] * keep_prev
        next_row = gn_ref[0:1, sl] * keep_next
        g_prev = jnp.where(row == 0, prev_row, pltpu.roll(g, 1, 0))
        g_next = jnp.where(row == tm - 1, next_row, pltpu.roll(g, tm - 1, 0))
        conv = g_prev * cw_ref[0:1, sl] + g * cw_ref[1:2, sl] + g_next * cw_ref[2:3, sl] + cb_ref[:, sl]
        gelu = 0.5 * conv * (1.0 + lax.erf(conv * (2.0 ** -0.5)))
        act_ref[:, sl] = (gelu * v_ref[:, sl]).astype(BF16)
    x2 = x_ref[...] + jnp.dot(act_ref[...], wd_ref[...], preferred_element_type=F32)
    ms = jnp.mean(x2 * x2, axis=-1, keepdims=True)
    o_ref[...] = x2 * lax.rsqrt(ms + EPS) * nf_ref[...]


def _ffn_out(up, x1, conv_w, conv_b, wd, norm_g, *, tm, S):
    T = x1.shape[0]
    tiles_per_seq = S // tm
    hb = tm // 8
    last_hb = T // 8 - 1
    return pl.pallas_call(
        functools.partial(_ffn_out_kernel, tiles_per_seq=tiles_per_seq),
        grid=(T // tm,),
        in_specs=[
            pl.BlockSpec((tm, D_FF), lambda i: (i, 0)),
            pl.BlockSpec((tm, D_FF), lambda i: (i, 1)),
            pl.BlockSpec((8, D_FF), lambda i: (jnp.maximum(i * hb - 1, 0), 0)),
            pl.BlockSpec((8, D_FF), lambda i: (jnp.minimum((i + 1) * hb, last_hb), 0)),
            pl.BlockSpec((tm, D_MODEL), lambda i: (i, 0)),
            pl.BlockSpec((3, D_FF), lambda i: (0, 0)),
            pl.BlockSpec((1, D_FF), lambda i: (0, 0)),
            pl.BlockSpec((D_FF, D_MODEL), lambda i: (0, 0)),
            pl.BlockSpec((1, D_MODEL), lambda i: (0, 0)),
        ],
        out_specs=pl.BlockSpec((tm, D_MODEL), lambda i: (i, 0)),
        out_shape=jax.ShapeDtypeStruct((T, D_MODEL), F32),
        scratch_shapes=[pltpu.VMEM((tm, D_FF), BF16)],
        compiler_params=_params(("parallel",), 56),
    )(up, up, up, up, x1, conv_w, conv_b.reshape(1, D_FF), wd, norm_g.reshape(1, D_MODEL))


def _trunk(x, w):
    B, S, _ = x.shape
    T = B * S
    x2d = x.reshape(T, D_MODEL)
    proj = _norm_matmul(x2d, w["norm_mix_g"], w["w_in"], tm=1024, tn=640)
    proj3 = proj.reshape(B, S, PROJ_COLS)
    o_att = _attention(proj3, *_rope_tables(S))
    o_dn = _deltanet(proj3, w["conv_qkv_w"], w["dn_par"], w["out_norm_g"])
    x1 = _mix(x2d, o_att.reshape(T, ATT_GROUP_WIDTH), o_dn.reshape(T, DN_WIDTH), proj,
              w["w_branch_a"], w["w_branch_b"], w["w_out"], tm=512)
    up = _norm_matmul(x1, w["norm_ffn_g"], w["w_up"], tm=1024, tn=512)
    y = _ffn_out(up, x1, w["ffn_conv_w"], w["ffn_conv_b"], w["w_down"], w["norm_final_g"], tm=512, S=S)
    return y.reshape(B, S, D_MODEL)


def kernel(x_prompt, x_sample, norm_mix_g, w_in, conv_qkv_w, a_log_f, a_log_b, dt_bias_f, dt_bias_b, out_norm_g, w_branch_a, w_branch_b, w_out, norm_ffn_g, w_up, ffn_conv_w, ffn_conv_b, w_down, norm_final_g):
    att = 3 * ATT_WIDTH
    dn_end = att + 3 * DN_WIDTH
    z_end = dn_end + DN_WIDTH
    small_end = z_end + 4 * DN_HEADS
    w_in_r = jnp.concatenate([
        w_in[:, small_end:],
        w_in[:, dn_end:z_end],
        w_in[:, att:dn_end],
        w_in[:, :att],
        w_in[:, z_end:small_end],
        jnp.zeros((D_MODEL, LANES - 4 * DN_HEADS), F32),
    ], axis=1).astype(BF16)
    par = jnp.zeros((8, LANES), F32)
    par = par.at[0, 2 * DN_HEADS:4 * DN_HEADS].set(jnp.concatenate([a_log_f, a_log_b]))
    par = par.at[1, 2 * DN_HEADS:4 * DN_HEADS].set(jnp.concatenate([dt_bias_f, dt_bias_b]))
    w = dict(
        norm_mix_g=norm_mix_g, w_in=w_in_r, conv_qkv_w=conv_qkv_w, dn_par=par, out_norm_g=out_norm_g,
        w_branch_a=w_branch_a.astype(BF16), w_branch_b=w_branch_b.astype(BF16), w_out=w_out.astype(BF16),
        norm_ffn_g=norm_ffn_g, w_up=w_up.astype(BF16), ffn_conv_w=ffn_conv_w, ffn_conv_b=ffn_conv_b,
        w_down=w_down.astype(BF16), norm_final_g=norm_final_g,
    )
    return _trunk(x_prompt, w), _trunk(x_sample, w)
```

```python
import functools

import jax
import jax.numpy as jnp
from jax import lax
from jax.experimental import pallas as pl
from jax.experimental.pallas import tpu as pltpu

F32 = jnp.float32
BF16 = jnp.bfloat16

D_MODEL = 1024
ATT_HEAD_DIM = 64
ATT_HEADS_PER_GROUP = 8
DILATIONS = (1, 4, 16)
ATT_HALF = 64
ATT_GROUP_WIDTH = ATT_HEADS_PER_GROUP * ATT_HEAD_DIM
ATT_WIDTH = len(DILATIONS) * ATT_GROUP_WIDTH
ROPE_DIM = ATT_HEAD_DIM // 4
ROPE_THETA = 500000.0
DN_HEADS = 8
DN_HEAD_DIM = 128
DN_WIDTH = DN_HEADS * DN_HEAD_DIM
SHORT_CONV = 5
D_FF = 2816
EPS = 1e-6
NEG_INF = -1e30

LANES = 128
MIB = 1024 * 1024

COL_GATE_A = 0
COL_GATE_B = 1024
COL_Z = 2048
COL_DN_Q = 3072
COL_DN_K = 4096
COL_DN_V = 5120
COL_ATT_Q = 6144
COL_ATT_K = COL_ATT_Q + ATT_WIDTH
COL_ATT_V = COL_ATT_K + ATT_WIDTH
PROJ_COLS = COL_ATT_V + ATT_WIDTH


def _params(semantics, vmem_mib):
    return pltpu.CompilerParams(dimension_semantics=semantics, vmem_limit_bytes=vmem_mib * MIB)


def _mm(a, b):
    return jnp.dot(a.astype(BF16), b.astype(BF16), preferred_element_type=F32)


def _mm_nt(a, b):
    return lax.dot_general(a.astype(BF16), b.astype(BF16), (((1,), (1,)), ((), ())),
                           preferred_element_type=F32)


def _mm_tn(a, b):
    return lax.dot_general(a.astype(BF16), b.astype(BF16), (((0,), (0,)), ((), ())),
                           preferred_element_type=F32)


def _norm_matmul_kernel(*refs, row_chunk, with_side):
    if with_side:
        x_ref, g_ref, w_ref, ws_ref, o_ref, os_ref, h_ref = refs
    else:
        x_ref, g_ref, w_ref, o_ref, h_ref = refs

    @pl.when(pl.program_id(1) == 0)
    def _():
        def body(c, carry):
            r0 = pl.multiple_of(c * row_chunk, row_chunk)
            x = x_ref[pl.ds(r0, row_chunk), :]
            ms = jnp.mean(x * x, axis=-1, keepdims=True)
            h_ref[pl.ds(r0, row_chunk), :] = (x * lax.rsqrt(ms + EPS) * g_ref[...]).astype(BF16)
            return carry
        lax.fori_loop(0, x_ref.shape[0] // row_chunk, body, 0)
        if with_side:
            os_ref[...] = jnp.dot(h_ref[...], ws_ref[...], preferred_element_type=F32)

    o_ref[...] = jnp.dot(h_ref[...], w_ref[...], preferred_element_type=F32).astype(o_ref.dtype)


def _norm_matmul(x2d, gain, w_bf16, w_side=None, *, tm, tn):
    T, K = x2d.shape
    N = w_bf16.shape[1]
    with_side = w_side is not None
    in_specs = [
        pl.BlockSpec((tm, K), lambda i, j: (i, 0)),
        pl.BlockSpec((1, K), lambda i, j: (0, 0)),
        pl.BlockSpec((K, tn), lambda i, j: (0, j)),
    ]
    out_specs = [pl.BlockSpec((tm, tn), lambda i, j: (i, j))]
    out_shape = [jax.ShapeDtypeStruct((T, N), BF16)]
    args = [x2d, gain.reshape(1, K), w_bf16]
    if with_side:
        ns = w_side.shape[1]
        in_specs.append(pl.BlockSpec((K, ns), lambda i, j: (0, 0)))
        out_specs.append(pl.BlockSpec((tm, ns), lambda i, j: (i, 0)))
        out_shape.append(jax.ShapeDtypeStruct((T, ns), F32))
        args.append(w_side)
    return pl.pallas_call(
        functools.partial(_norm_matmul_kernel, row_chunk=128, with_side=with_side),
        grid=(T // tm, N // tn),
        in_specs=in_specs,
        out_specs=out_specs,
        out_shape=out_shape,
        scratch_shapes=[pltpu.VMEM((tm, K), BF16)],
        compiler_params=_params(("parallel", "arbitrary"), 48),
    )(*args)


ATT_QB = 128
ATT_UNROLL = 4


def _attn_group(q_ref, k_ref, v_ref, cos_ref, sa_ref, sb_ref, x32, qs, ks, vs, acc_ref, m_ref, l_ref,
                *, S, dil, first):
    L = S // dil
    rc = min(L, 256)
    nlc = L // rc

    def rows_of(start, n):
        return pl.ds(start, n, stride=dil) if dil > 1 else pl.ds(pl.multiple_of(start, ATT_QB), n)

    if dil > 1:
        def widen(c, carry):
            sl = pl.ds(pl.multiple_of(c * 256, 256), 256)
            for a, ref in enumerate((q_ref, k_ref, v_ref)):
                x32[a, sl, :] = ref[0, sl, :].astype(F32)
            return carry

        lax.fori_loop(0, S // 256, widen, 0)
        load = lambda a, rows: x32[a, rows, :]
    else:
        load = lambda a, rows: (q_ref, k_ref, v_ref)[a][0, rows, :].astype(F32)

    def pre(idx, carry):
        r = idx // nlc if dil > 1 else 0
        lc = idx % nlc
        rows = rows_of(r + dil * lc * rc, rc)
        pos = r * L + lc * rc
        cs = cos_ref[rows, :]
        sa = sa_ref[rows, :]
        sb = sb_ref[rows, :]

        def rope(t):
            return t * cs + pltpu.roll(t, LANES - ROPE_DIM // 2, 1) * sa + pltpu.roll(t, ROPE_DIM // 2, 1) * sb

        q = rope(load(0, rows)) * (ATT_HEAD_DIM ** -0.5)
        ks[pl.ds(pl.multiple_of(pos, rc), rc), :] = rope(load(1, rows)).astype(BF16)
        vs[pl.ds(pl.multiple_of(pos, rc), rc), :] = load(2, rows).astype(BF16)
        head_a = lax.broadcasted_iota(jnp.int32, (rc, LANES), 1) < ATT_HEAD_DIM
        qa = jnp.where(head_a, q, 0.0).astype(BF16)
        qb = jnp.where(head_a, 0.0, q).astype(BF16)
        for sub in range(rc // ATT_QB):
            dst = pl.multiple_of(2 * pos + 2 * sub * ATT_QB, 2 * ATT_QB)
            qs[pl.ds(dst, ATT_QB), :] = qa[sub * ATT_QB:(sub + 1) * ATT_QB]
            qs[pl.ds(dst + ATT_QB, ATT_QB), :] = qb[sub * ATT_QB:(sub + 1) * ATT_QB]
        return carry

    lax.fori_loop(0, dil * nlc, pre, 0)

    nb = L // ATT_QB
    nk = min(2 * ATT_QB, L)
    col = lax.broadcasted_iota(jnp.int32, (ATT_QB, nk), 1)
    row = lax.broadcasted_iota(jnp.int32, (ATT_QB, nk), 0)
    head_a = lax.broadcasted_iota(jnp.int32, (ATT_QB, LANES), 1) < ATT_HEAD_DIM

    def both(t):
        return jnp.where(head_a, t[:ATT_QB], t[ATT_QB:])

    def blk(it, carry):
        work = []
        for ub in range(ATT_UNROLL):
            idx = it * ATT_UNROLL + ub
            r = idx // nb if dil > 1 else 0
            n = idx % nb
            q0 = n * ATT_QB
            start = jnp.clip(q0 - ATT_HALF, 0, L - nk)
            valid = jnp.abs(col - row + (start - q0)) <= ATT_HALF
            qsl = pl.ds(pl.multiple_of(2 * (r * L + q0), 2 * ATT_QB), 2 * ATT_QB)
            ksl = pl.ds(pl.multiple_of(r * L + start, ATT_HALF), nk)
            s = lax.dot_general(qs[qsl, :], ks[ksl, :], (((1,), (1,)), ((), ())), preferred_element_type=F32)
            work.append(dict(rows=rows_of(r + dil * q0, ATT_QB), ksl=ksl, s=s,
                             valid=jnp.concatenate([valid, valid], axis=0)))
        for w in work:
            s = jnp.where(w["valid"], w["s"], NEG_INF)
            w["m"] = jnp.max(s, axis=1, keepdims=True)
            p = jnp.exp(s - w["m"])
            w["l"] = jnp.sum(p, axis=1, keepdims=True)
            w["p"] = p.astype(BF16)
        for w in work:
            w["pv"] = jnp.dot(w["p"], vs[w["ksl"], :], preferred_element_type=F32)
        for w in work:
            rows, m, l, pv = w["rows"], both(w["m"]), both(w["l"]), both(w["pv"])
            if first:
                m_ref[rows, :] = m
                l_ref[rows, :] = l
                acc_ref[rows, :] = pv
            else:
                m_old = m_ref[rows, :]
                m_new = jnp.maximum(m_old, m)
                a = jnp.exp(m_old - m_new)
                b = jnp.exp(m - m_new)
                m_ref[rows, :] = m_new
                l_ref[rows, :] = a * l_ref[rows, :] + b * l
                acc_ref[rows, :] = a * acc_ref[rows, :] + b * pv
        return carry

    lax.fori_loop(0, dil * nb // ATT_UNROLL, blk, 0)


def _attn_kernel(q_ref, k_ref, v_ref, cos_ref, sa_ref, sb_ref, o_ref,
                 x32, qs, ks, vs, acc_ref, m_ref, l_ref, *, S):
    g = pl.program_id(2)
    for gi, dil in enumerate(DILATIONS):
        @pl.when(g == gi)
        def _(gi=gi, dil=dil):
            _attn_group(q_ref, k_ref, v_ref, cos_ref, sa_ref, sb_ref, x32, qs, ks, vs, acc_ref, m_ref, l_ref,
                        S=S, dil=dil, first=(gi == 0))

    @pl.when(g == len(DILATIONS) - 1)
    def _():
        rc = 256

        def fin(c, carry):
            sl = pl.ds(pl.multiple_of(c * rc, rc), rc)
            o_ref[0, sl, :] = (acc_ref[sl, :] / l_ref[sl, :]).astype(o_ref.dtype)
            return carry

        lax.fori_loop(0, S // rc, fin, 0)


def _attention(proj3, cos_t, sa_t, sb_t):
    B, S, _ = proj3.shape
    ng = len(DILATIONS)

    def col_spec(col0):
        base = col0 // LANES
        per_group = ATT_GROUP_WIDTH // LANES
        return pl.BlockSpec((1, S, LANES), lambda b, j, g: (b, 0, base + g * per_group + j))

    tab_spec = pl.BlockSpec((S, LANES), lambda b, j, g: (0, 0))
    seq_bf16 = pltpu.VMEM((S, LANES), BF16)
    seq_f32 = pltpu.VMEM((S, LANES), F32)
    return pl.pallas_call(
        functools.partial(_attn_kernel, S=S),
        grid=(B, ATT_GROUP_WIDTH // LANES, ng),
        in_specs=[col_spec(COL_ATT_Q), col_spec(COL_ATT_K), col_spec(COL_ATT_V), tab_spec, tab_spec, tab_spec],
        out_specs=pl.BlockSpec((1, S, LANES), lambda b, j, g: (b, 0, j)),
        out_shape=jax.ShapeDtypeStruct((B, S, ATT_GROUP_WIDTH), BF16),
        scratch_shapes=[pltpu.VMEM((3, S, LANES), F32),
                        pltpu.VMEM((2 * S, LANES), BF16), seq_bf16, seq_bf16, seq_f32, seq_f32, seq_f32],
        compiler_params=_params(("parallel", "parallel", "arbitrary"), 48),
    )(proj3, proj3, proj3, cos_t, sa_t, sb_t)


def _rope_tables(S):
    half = ROPE_DIM // 2
    inv = ROPE_THETA ** (-jnp.arange(half, dtype=F32) / half)
    ang = jnp.arange(S, dtype=F32)[:, None] * inv[None, :]
    cos, sin = jnp.cos(ang), jnp.sin(ang)
    pad = ATT_HEAD_DIM - ROPE_DIM
    ones = jnp.ones((S, pad), F32)
    zeros = jnp.zeros((S, pad), F32)
    zh = jnp.zeros((S, half), F32)
    cos_h = jnp.concatenate([cos, cos, ones], axis=1)
    sa_h = jnp.concatenate([-sin, zh, zeros], axis=1)
    sb_h = jnp.concatenate([zh, sin, zeros], axis=1)
    tile = lambda t: jnp.concatenate([t, t], axis=1)
    return tile(cos_h), tile(sa_h), tile(sb_h)


DN_BLK = 128
DN_UNROLL = 4
DN_DOUBLINGS = 5


def _dn_kernel(q_ref, k_ref, v_ref, z_ref, sm_ref, cwq_ref, cwk_ref, cwv_ref, par_ref, ng_ref, o_ref,
               xp, qn, kn, vn, kq_s, b_s, dec_s, oacc, *, S):
    h = pl.program_id(1)
    rc = 256
    halo = 8

    xp[0:halo, :] = jnp.zeros((halo, LANES), F32)
    xp[S + halo:S + 2 * halo, :] = jnp.zeros((halo, LANES), F32)
    for src, cw, dst, mode in ((q_ref, cwq_ref, qn, "q"), (k_ref, cwk_ref, kn, "k"), (v_ref, cwv_ref, vn, "v")):
        def cp(c, carry, src=src):
            r0 = pl.multiple_of(c * rc, rc)
            xp[pl.ds(r0 + halo, rc), :] = src[0, pl.ds(r0, rc), :].astype(F32)
            return carry

        lax.fori_loop(0, S // rc, cp, 0)

        def cv(c, carry, cw=cw, dst=dst, mode=mode):
            r0 = pl.multiple_of(c * rc, rc)
            win = xp[pl.ds(r0, rc + 2 * halo), :]
            off = halo - SHORT_CONV // 2
            y = win[off:off + rc] * cw[0:1, :]
            for i in range(1, SHORT_CONV):
                y = y + win[off + i:off + i + rc] * cw[i:i + 1, :]
            y = y * jax.nn.sigmoid(y)
            if mode != "v":
                y = y * lax.rsqrt(jnp.sum(y * y, axis=-1, keepdims=True) + EPS)
            if mode == "q":
                y = y * (DN_HEAD_DIM ** -0.5)
            dst[pl.ds(r0, rc), :] = y
            return carry

        lax.fori_loop(0, S // rc, cv, 0)

    lane = lax.broadcasted_iota(jnp.int32, (DN_BLK, DN_BLK), 1)
    rowi = lax.broadcasted_iota(jnp.int32, (DN_BLK, DN_BLK), 0)
    eye = jnp.where(lane == rowi, 1.0, 0.0)
    same_half = (rowi >= DN_BLK // 2) == (lane >= DN_BLK // 2)
    a_exp = jnp.exp(par_ref[0:1, :])
    dt_bias = par_ref[1:2, :]
    n_blocks = S // DN_BLK

    def masks(d):
        return (lane <= rowi, lane < rowi) if d == 0 else (lane >= rowi, lane > rowi)

    tri3 = [jnp.concatenate([jnp.where(masks(d)[0], 1.0, 0.0).astype(BF16)] * 3, axis=1) for d in range(2)]

    def prepass(c, carry):
        chains = []
        for uu in range(DN_UNROLL):
            blk = c * DN_UNROLL + uu
            rows = pl.ds(pl.multiple_of(blk * DN_BLK, DN_BLK), DN_BLK)
            sm = sm_ref[0, rows, :]
            beta_all = jax.nn.sigmoid(sm)
            g_all = -a_exp * jax.nn.softplus(sm + dt_bias)
            q = qn[rows, :]
            k = kn[rows, :]
            v = vn[rows, :]
            qk = _mm_nt(q, k)
            for d in range(2):
                beta = jnp.sum(jnp.where(lane == h + DN_HEADS * d, beta_all, 0.0), axis=1, keepdims=True)
                g = jnp.sum(jnp.where(lane == h + DN_HEADS * (2 + d), g_all, 0.0), axis=1, keepdims=True)
                gb = jnp.broadcast_to(g, (DN_BLK, DN_BLK))
                hi = gb.astype(BF16)
                r1 = gb - hi.astype(F32)
                mid = r1.astype(BF16)
                lo = (r1 - mid.astype(F32)).astype(BF16)
                chains.append(dict(blk=blk, rows=rows, d=d, q=q, k=k, qk=qk, kb=k * beta, vb=v * beta,
                                   g3=jnp.concatenate([hi, mid, lo], axis=0)))
        for ch in chains:
            d = ch["d"]
            tri = masks(d)[0]
            gc = jnp.dot(tri3[d], ch["g3"], preferred_element_type=F32)
            last = DN_BLK - 1 if d == 0 else 0
            ch["gc"], ch["gl"] = gc, gc[last:last + 1, :]
            ch["dm"] = jnp.where(tri, jnp.exp(jnp.where(tri, gc - gc.T, 0.0)), 0.0)
        for ch in chains:
            nm = jnp.where(masks(ch["d"])[1], -(_mm_nt(ch["kb"], ch["k"]) * ch["dm"]), 0.0)
            ch["nd"] = jnp.where(same_half, nm, 0.0)
            ch["no"] = jnp.where(same_half, 0.0, nm)
        for ch in chains:
            ch["x"] = eye + ch["nd"]
            ch["p"] = _mm(ch["nd"], ch["nd"])
        for _ in range(DN_DOUBLINGS - 1):
            for ch in chains:
                x_next = ch["x"] + _mm(ch["x"], ch["p"])
                ch["p"] = _mm(ch["p"], ch["p"])
                ch["x"] = x_next
        for ch in chains:
            ch["x"] = ch["x"] + _mm(ch["x"], ch["p"])
            ch["y"] = _mm(ch["no"], ch["x"])
        for ch in chains:
            eg = jnp.exp(ch["gc"])
            x = ch["x"] + _mm(ch["x"], ch["y"])
            ch["wu"] = _mm(x, jnp.concatenate([ch["kb"] * eg, ch["vb"]], axis=1))
            ch["qg"] = ch["q"] * eg
            ch["kg"] = ch["k"] * jnp.exp(ch["gl"] - ch["gc"])
        for ch in chains:
            iw = _mm(ch["qk"] * ch["dm"], ch["wu"])
            ch["qp"] = ch["qg"] - iw[:, :LANES]
            ch["oc"] = iw[:, LANES:]
        for ch in chains:
            d, blk = ch["d"], ch["blk"]
            kb_ = _mm_tn(ch["kg"], ch["wu"])
            base = pl.multiple_of(blk * 2 * DN_BLK, 2 * DN_BLK)
            kq_s[d, pl.ds(base, DN_BLK), :] = (-kb_[:, :LANES]).astype(BF16)
            kq_s[d, pl.ds(base + DN_BLK, DN_BLK), :] = ch["qp"].astype(BF16)
            b_s[d, ch["rows"], :] = kb_[:, LANES:]
            dec_s[d, pl.ds(pl.multiple_of(blk * 8, 8), 8), :] = jnp.broadcast_to(jnp.exp(ch["gl"]), (8, LANES))
        for uu in range(DN_UNROLL):
            oacc[chains[2 * uu]["rows"], :] = chains[2 * uu]["oc"] + chains[2 * uu + 1]["oc"]
        return carry

    lax.fori_loop(0, n_blocks // DN_UNROLL, prepass, 0)

    def step(d, blk, state):
        base = pl.multiple_of(blk * 2 * DN_BLK, 2 * DN_BLK)
        r = jnp.dot(kq_s[d, pl.ds(base, 2 * DN_BLK), :], state.astype(BF16), preferred_element_type=F32)
        rows = pl.ds(pl.multiple_of(blk * DN_BLK, DN_BLK), DN_BLK)
        oacc[rows, :] += r[DN_BLK:]
        dec = dec_s[d, pl.ds(pl.multiple_of(blk * 8, 8), 1), :]
        return state * dec + r[:DN_BLK] + b_s[d, rows, :]

    def rec(t, carry):
        sf, sb = carry
        bf = 2 * t
        bb = n_blocks - 1 - 2 * t
        sf = step(0, bf, sf)
        sb = step(1, bb, sb)
        sf = step(0, bf + 1, sf)
        sb = step(1, bb - 1, sb)
        return sf, sb

    zero_state = jnp.zeros((DN_HEAD_DIM, DN_HEAD_DIM), F32)
    lax.fori_loop(0, n_blocks // 2, rec, (zero_state, zero_state))

    def fin(c, carry):
        sl = pl.ds(pl.multiple_of(c * rc, rc), rc)
        o = oacc[sl, :]
        o = o * lax.rsqrt(jnp.mean(o * o, axis=-1, keepdims=True) + EPS) * ng_ref[...]
        z = z_ref[0, sl, :].astype(F32)
        o_ref[0, sl, :] = (o * (z * jax.nn.sigmoid(z))).astype(o_ref.dtype)
        return carry

    lax.fori_loop(0, S // rc, fin, 0)


def _deltanet(proj3, small3, conv_w, par, norm_g):
    B, S, _ = proj3.shape

    def col_spec(col0):
        base = col0 // LANES
        return pl.BlockSpec((1, S, LANES), lambda b, h: (b, 0, base + h))

    def conv_spec(part):
        return pl.BlockSpec((SHORT_CONV, LANES), lambda b, h: (0, part * DN_HEADS + h))

    seq_f32 = pltpu.VMEM((S, LANES), F32)
    return pl.pallas_call(
        functools.partial(_dn_kernel, S=S),
        grid=(B, DN_HEADS),
        in_specs=[col_spec(COL_DN_Q), col_spec(COL_DN_K), col_spec(COL_DN_V), col_spec(COL_Z),
                  pl.BlockSpec((1, S, LANES), lambda b, h: (b, 0, 0)),
                  conv_spec(0), conv_spec(1), conv_spec(2),
                  pl.BlockSpec((8, LANES), lambda b, h: (0, 0)),
                  pl.BlockSpec((1, LANES), lambda b, h: (0, 0))],
        out_specs=pl.BlockSpec((1, S, LANES), lambda b, h: (b, 0, h)),
        out_shape=jax.ShapeDtypeStruct((B, S, DN_WIDTH), BF16),
        scratch_shapes=[pltpu.VMEM((S + 16, LANES), F32), seq_f32, seq_f32, seq_f32,
                        pltpu.VMEM((2, 2 * S, LANES), BF16),
                        pltpu.VMEM((2, S, LANES), F32),
                        pltpu.VMEM((2, S // DN_BLK * 8, LANES), F32),
                        seq_f32],
        compiler_params=_params(("parallel", "parallel"), 56),
    )(proj3, proj3, proj3, proj3, small3, conv_w, conv_w, conv_w, par, norm_g.reshape(1, LANES))


def _mix_kernel(x_ref, oa_ref, ob_ref, ga_ref, gb_ref, wa_ref, wb_ref, wo_ref, o_ref):
    ya = jnp.dot(oa_ref[...], wa_ref[...], preferred_element_type=F32)
    yb = jnp.dot(ob_ref[...], wb_ref[...], preferred_element_type=F32)
    mix = jax.nn.sigmoid(ga_ref[...].astype(F32)) * ya + jax.nn.sigmoid(gb_ref[...].astype(F32)) * yb
    o_ref[...] = x_ref[...] + jnp.dot(mix.astype(BF16), wo_ref[...], preferred_element_type=F32)


def _mix(x2d, o_att, o_dn, proj, wa, wb, wo, *, tm):
    T = x2d.shape[0]
    row = lambda width, cb=0: pl.BlockSpec((tm, width), lambda i: (i, cb))
    full = lambda a: pl.BlockSpec(a.shape, lambda i: (0, 0))
    return pl.pallas_call(
        _mix_kernel,
        grid=(T // tm,),
        in_specs=[row(D_MODEL), row(ATT_GROUP_WIDTH), row(DN_WIDTH),
                  row(D_MODEL, COL_GATE_A // D_MODEL), row(D_MODEL, COL_GATE_B // D_MODEL),
                  full(wa), full(wb), full(wo)],
        out_specs=row(D_MODEL),
        out_shape=jax.ShapeDtypeStruct((T, D_MODEL), F32),
        compiler_params=_params(("parallel",), 48),
    )(x2d, o_att, o_dn, proj, proj, wa, wb, wo)


FFN_KC = 256
FFN_HALO = 16


def _ffn_out_kernel(g_ref, v_ref, gp_ref, gn_ref, x_ref, cw_ref, cb_ref, wd_ref, nf_ref, o_ref, act_ref,
                    *, tiles_per_seq):
    i = pl.program_id(0)
    tm = g_ref.shape[0]
    pos = i % tiles_per_seq
    keep_prev = jnp.where(pos == 0, 0.0, 1.0)
    keep_next = jnp.where(pos == tiles_per_seq - 1, 0.0, 1.0)
    row = lax.broadcasted_iota(jnp.int32, (tm, FFN_KC), 0)
    for c in range(D_FF // FFN_KC):
        sl = slice(c * FFN_KC, (c + 1) * FFN_KC)
        g = g_ref[:, sl].astype(F32)
        prev_row = gp_ref[FFN_HALO - 1:FFN_HALO, sl].astype(F32) * keep_prev
        next_row = gn_ref[0:1, sl].astype(F32) * keep_next
        g_prev = jnp.where(row == 0, prev_row, pltpu.roll(g, 1, 0))
        g_next = jnp.where(row == tm - 1, next_row, pltpu.roll(g, tm - 1, 0))
        conv = g_prev * cw_ref[0:1, sl] + g * cw_ref[1:2, sl] + g_next * cw_ref[2:3, sl] + cb_ref[:, sl]
        gelu = 0.5 * conv * (1.0 + lax.erf(conv * (2.0 ** -0.5)))
        act_ref[:, sl] = (gelu * v_ref[:, sl].astype(F32)).astype(BF16)
    x2 = x_ref[...] + jnp.dot(act_ref[...], wd_ref[...], preferred_element_type=F32)
    ms = jnp.mean(x2 * x2, axis=-1, keepdims=True)
    o_ref[...] = x2 * lax.rsqrt(ms + EPS) * nf_ref[...]


def _ffn_out(up, x1, conv_w, conv_b, wd, norm_g, *, tm, S):
    T = x1.shape[0]
    tiles_per_seq = S // tm
    hb = tm // FFN_HALO
    last_hb = T // FFN_HALO - 1
    return pl.pallas_call(
        functools.partial(_ffn_out_kernel, tiles_per_seq=tiles_per_seq),
        grid=(T // tm,),
        in_specs=[
            pl.BlockSpec((tm, D_FF), lambda i: (i, 0)),
            pl.BlockSpec((tm, D_FF), lambda i: (i, 1)),
            pl.BlockSpec((FFN_HALO, D_FF), lambda i: (jnp.maximum(i * hb - 1, 0), 0)),
            pl.BlockSpec((FFN_HALO, D_FF), lambda i: (jnp.minimum((i + 1) * hb, last_hb), 0)),
            pl.BlockSpec((tm, D_MODEL), lambda i: (i, 0)),
            pl.BlockSpec((3, D_FF), lambda i: (0, 0)),
            pl.BlockSpec((1, D_FF), lambda i: (0, 0)),
            pl.BlockSpec((D_FF, D_MODEL), lambda i: (0, 0)),
            pl.BlockSpec((1, D_MODEL), lambda i: (0, 0)),
        ],
        out_specs=pl.BlockSpec((tm, D_MODEL), lambda i: (i, 0)),
        out_shape=jax.ShapeDtypeStruct((T, D_MODEL), F32),
        scratch_shapes=[pltpu.VMEM((tm, D_FF), BF16)],
        compiler_params=_params(("parallel",), 56),
    )(up, up, up, up, x1, conv_w, conv_b.reshape(1, D_FF), wd, norm_g.reshape(1, D_MODEL))


def _trunk(x, w):
    B, S, _ = x.shape
    T = B * S
    x2d = x.reshape(T, D_MODEL)
    proj, small = _norm_matmul(x2d, w["norm_mix_g"], w["w_in"], w["w_small"], tm=1024, tn=1536)
    proj3 = proj.reshape(B, S, PROJ_COLS)
    o_att = _attention(proj3, *_rope_tables(S))
    o_dn = _deltanet(proj3, small.reshape(B, S, LANES), w["conv_qkv_w"], w["dn_par"], w["out_norm_g"])
    x1 = _mix(x2d, o_att.reshape(T, ATT_GROUP_WIDTH), o_dn.reshape(T, DN_WIDTH), proj,
              w["w_branch_a"], w["w_branch_b"], w["w_out"], tm=512)
    up = _norm_matmul(x1, w["norm_ffn_g"], w["w_up"], tm=1024, tn=D_FF)[0]
    y = _ffn_out(up, x1, w["ffn_conv_w"], w["ffn_conv_b"], w["w_down"], w["norm_final_g"], tm=512, S=S)
    return y.reshape(B, S, D_MODEL)


def kernel(x_prompt, x_sample, norm_mix_g, w_in, conv_qkv_w, a_log_f, a_log_b, dt_bias_f, dt_bias_b, out_norm_g, w_branch_a, w_branch_b, w_out, norm_ffn_g, w_up, ffn_conv_w, ffn_conv_b, w_down, norm_final_g):
    att = 3 * ATT_WIDTH
    dn_end = att + 3 * DN_WIDTH
    z_end = dn_end + DN_WIDTH
    small_end = z_end + 4 * DN_HEADS
    w_in_r = jnp.concatenate([
        w_in[:, small_end:],
        w_in[:, dn_end:z_end],
        w_in[:, att:dn_end],
        w_in[:, :att],
    ], axis=1).astype(BF16)
    w_small = jnp.pad(w_in[:, z_end:small_end], ((0, 0), (0, LANES - 4 * DN_HEADS))).astype(BF16)
    par = jnp.zeros((8, LANES), F32)
    par = par.at[0, 2 * DN_HEADS:4 * DN_HEADS].set(jnp.concatenate([a_log_f, a_log_b]))
    par = par.at[1, 2 * DN_HEADS:4 * DN_HEADS].set(jnp.concatenate([dt_bias_f, dt_bias_b]))
    w = dict(
        norm_mix_g=norm_mix_g, w_in=w_in_r, w_small=w_small, conv_qkv_w=conv_qkv_w, dn_par=par, out_norm_g=out_norm_g,
        w_branch_a=w_branch_a.astype(BF16), w_branch_b=w_branch_b.astype(BF16), w_out=w_out.astype(BF16),
        norm_ffn_g=norm_ffn_g, w_up=w_up.astype(BF16), ffn_conv_w=ffn_conv_w, ffn_conv_b=ffn_conv_b,
        w_down=w_down.astype(BF16), norm_final_g=norm_final_g,
    )
    return _trunk(x_prompt, w), _trunk(x_sample, w)
```

```python
import functools

import jax
import jax.numpy as jnp
from jax import lax
from jax.experimental import pallas as pl
from jax.experimental.pallas import tpu as pltpu

F32 = jnp.float32
BF16 = jnp.bfloat16

D_MODEL = 1024
ATT_HEAD_DIM = 64
ATT_HEADS_PER_GROUP = 8
DILATIONS = (1, 4, 16)
ATT_HALF = 64
ATT_GROUP_WIDTH = ATT_HEADS_PER_GROUP * ATT_HEAD_DIM
ATT_WIDTH = len(DILATIONS) * ATT_GROUP_WIDTH
ROPE_DIM = ATT_HEAD_DIM // 4
ROPE_THETA = 500000.0
DN_HEADS = 8
DN_HEAD_DIM = 128
DN_WIDTH = DN_HEADS * DN_HEAD_DIM
SHORT_CONV = 5
D_FF = 2816
EPS = 1e-6
NEG_INF = -1e30

LANES = 128
MIB = 1024 * 1024

COL_GATE_A = 0
COL_GATE_B = 1024
COL_Z = 2048
COL_DN_Q = 3072
COL_DN_K = 4096
COL_DN_V = 5120
COL_ATT_Q = 6144
COL_ATT_K = COL_ATT_Q + ATT_WIDTH
COL_ATT_V = COL_ATT_K + ATT_WIDTH
PROJ_COLS = COL_ATT_V + ATT_WIDTH


def _params(semantics, vmem_mib):
    return pltpu.CompilerParams(dimension_semantics=semantics, vmem_limit_bytes=vmem_mib * MIB)


def _mm(a, b):
    return jnp.dot(a.astype(BF16), b.astype(BF16), preferred_element_type=F32)


def _mm_nt(a, b):
    return lax.dot_general(a.astype(BF16), b.astype(BF16), (((1,), (1,)), ((), ())),
                           preferred_element_type=F32)


def _mm_tn(a, b):
    return lax.dot_general(a.astype(BF16), b.astype(BF16), (((0,), (0,)), ((), ())),
                           preferred_element_type=F32)


def _norm_matmul_kernel(*refs, row_chunk, with_side):
    if with_side:
        x_ref, g_ref, w_ref, ws_ref, o_ref, os_ref, h_ref = refs
    else:
        x_ref, g_ref, w_ref, o_ref, h_ref = refs

    @pl.when(pl.program_id(1) == 0)
    def _():
        def body(c, carry):
            r0 = pl.multiple_of(c * row_chunk, row_chunk)
            x = x_ref[pl.ds(r0, row_chunk), :]
            ms = jnp.mean(x * x, axis=-1, keepdims=True)
            h_ref[pl.ds(r0, row_chunk), :] = (x * lax.rsqrt(ms + EPS) * g_ref[...]).astype(BF16)
            return carry
        lax.fori_loop(0, x_ref.shape[0] // row_chunk, body, 0)
        if with_side:
            os_ref[...] = jnp.dot(h_ref[...], ws_ref[...], preferred_element_type=F32)

    o_ref[...] = jnp.dot(h_ref[...], w_ref[...], preferred_element_type=F32).astype(o_ref.dtype)


def _norm_matmul(x2d, gain, w_bf16, w_side=None, *, tm, tn):
    T, K = x2d.shape
    N = w_bf16.shape[1]
    with_side = w_side is not None
    in_specs = [
        pl.BlockSpec((tm, K), lambda i, j: (i, 0)),
        pl.BlockSpec((1, K), lambda i, j: (0, 0)),
        pl.BlockSpec((K, tn), lambda i, j: (0, j)),
    ]
    out_specs = [pl.BlockSpec((tm, tn), lambda i, j: (i, j))]
    out_shape = [jax.ShapeDtypeStruct((T, N), BF16)]
    args = [x2d, gain.reshape(1, K), w_bf16]
    if with_side:
        ns = w_side.shape[1]
        in_specs.append(pl.BlockSpec((K, ns), lambda i, j: (0, 0)))
        out_specs.append(pl.BlockSpec((tm, ns), lambda i, j: (i, 0)))
        out_shape.append(jax.ShapeDtypeStruct((T, ns), F32))
        args.append(w_side)
    return pl.pallas_call(
        functools.partial(_norm_matmul_kernel, row_chunk=128, with_side=with_side),
        grid=(T // tm, N // tn),
        in_specs=in_specs,
        out_specs=out_specs,
        out_shape=out_shape,
        scratch_shapes=[pltpu.VMEM((tm, K), BF16)],
        compiler_params=_params(("parallel", "arbitrary"), 48),
    )(*args)


ATT_QB = 128
ATT_UNROLL = 4


def _attn_group(q_ref, k_ref, v_ref, cos_ref, sin_ref, x32, qs, ks, vs, acc_ref, m_ref, l_ref,
                *, S, dil, first):
    L = S // dil
    rc = min(L, 256)
    nlc = L // rc

    def rows_of(start, n):
        return pl.ds(start, n, stride=dil) if dil > 1 else pl.ds(pl.multiple_of(start, ATT_QB), n)

    if dil > 1:
        def widen(c, carry):
            sl = pl.ds(pl.multiple_of(c * 256, 256), 256)
            for a, ref in enumerate((q_ref, k_ref, v_ref)):
                x32[a, sl, :] = ref[0, sl, :].astype(F32)
            return carry

        lax.fori_loop(0, S // 256, widen, 0)
        load = lambda a, rows: x32[a, rows, :]
    else:
        load = lambda a, rows: (q_ref, k_ref, v_ref)[a][0, rows, :].astype(F32)

    lane_in = lax.broadcasted_iota(jnp.int32, (LANES, LANES), 0)
    lane_out = lax.broadcasted_iota(jnp.int32, (LANES, LANES), 1)
    head_lane = lane_out % ATT_HEAD_DIM
    half = ROPE_DIM // 2
    rot_m = jnp.where((head_lane < half) & (lane_in == lane_out + half), -1.0,
                      jnp.where((head_lane >= half) & (head_lane < ROPE_DIM) & (lane_in == lane_out - half),
                                1.0, 0.0)).astype(BF16)

    def pre(idx, carry):
        r = idx // nlc if dil > 1 else 0
        lc = idx % nlc
        rows = rows_of(r + dil * lc * rc, rc)
        pos = r * L + lc * rc
        cs = cos_ref[rows, :]
        sn = sin_ref[rows, :]

        def rope(t):
            return t * cs + jnp.dot(t.astype(BF16), rot_m, preferred_element_type=F32) * sn

        q = rope(load(0, rows))
        ks[pl.ds(pl.multiple_of(pos, rc), rc), :] = rope(load(1, rows)).astype(BF16)
        vs[pl.ds(pl.multiple_of(pos, rc), rc), :] = load(2, rows).astype(BF16)
        head_a = lax.broadcasted_iota(jnp.int32, (rc, LANES), 1) < ATT_HEAD_DIM
        qa = jnp.where(head_a, q, 0.0).astype(BF16)
        qb = jnp.where(head_a, 0.0, q).astype(BF16)
        for sub in range(rc // ATT_QB):
            dst = pl.multiple_of(2 * pos + 2 * sub * ATT_QB, 2 * ATT_QB)
            qs[pl.ds(dst, ATT_QB), :] = qa[sub * ATT_QB:(sub + 1) * ATT_QB]
            qs[pl.ds(dst + ATT_QB, ATT_QB), :] = qb[sub * ATT_QB:(sub + 1) * ATT_QB]
        return carry

    lax.fori_loop(0, dil * nlc, pre, 0, unroll=2)

    nb = L // ATT_QB
    nk = min(2 * ATT_QB, L)
    col = lax.broadcasted_iota(jnp.int32, (ATT_QB, nk), 1)
    row = lax.broadcasted_iota(jnp.int32, (ATT_QB, nk), 0)
    head_a = lax.broadcasted_iota(jnp.int32, (ATT_QB, LANES), 1) < ATT_HEAD_DIM

    def both(t):
        return jnp.where(head_a, t[:ATT_QB], t[ATT_QB:])

    def blk(it, carry):
        work = []
        for ub in range(ATT_UNROLL):
            idx = it * ATT_UNROLL + ub
            r = idx // nb if dil > 1 else 0
            n = idx % nb
            q0 = n * ATT_QB
            start = jnp.clip(q0 - ATT_HALF, 0, L - nk)
            valid = jnp.abs(col - row + (start - q0)) <= ATT_HALF
            qsl = pl.ds(pl.multiple_of(2 * (r * L + q0), 2 * ATT_QB), 2 * ATT_QB)
            ksl = pl.ds(pl.multiple_of(r * L + start, ATT_HALF), nk)
            s = lax.dot_general(qs[qsl, :], ks[ksl, :], (((1,), (1,)), ((), ())), preferred_element_type=F32)
            work.append(dict(rows=rows_of(r + dil * q0, ATT_QB), ksl=ksl, s=s,
                             valid=jnp.concatenate([valid, valid], axis=0)))
        for w in work:
            s = jnp.where(w["valid"], w["s"], NEG_INF)
            w["m"] = jnp.max(s, axis=1, keepdims=True)
            p = jnp.exp(s - w["m"])
            w["l"] = jnp.sum(p, axis=1, keepdims=True)
            w["p"] = p.astype(BF16)
        for w in work:
            w["pv"] = jnp.dot(w["p"], vs[w["ksl"], :], preferred_element_type=F32)
        for w in work:
            rows, m, l, pv = w["rows"], both(w["m"]), both(w["l"]), both(w["pv"])
            if first:
                m_ref[rows, :] = m
                l_ref[rows, :] = l
                acc_ref[rows, :] = pv
            else:
                m_old = m_ref[rows, :]
                m_new = jnp.maximum(m_old, m)
                a = jnp.exp(m_old - m_new)
                b = jnp.exp(m - m_new)
                m_ref[rows, :] = m_new
                l_ref[rows, :] = a * l_ref[rows, :] + b * l
                acc_ref[rows, :] = a * acc_ref[rows, :] + b * pv
        return carry

    lax.fori_loop(0, dil * nb // ATT_UNROLL, blk, 0)


def _attn_kernel(q_ref, k_ref, v_ref, cos_ref, sin_ref, o_ref,
                 x32, qs, ks, vs, acc_ref, m_ref, l_ref, *, S):
    g = pl.program_id(2)
    for gi, dil in enumerate(DILATIONS):
        @pl.when(g == gi)
        def _(gi=gi, dil=dil):
            _attn_group(q_ref, k_ref, v_ref, cos_ref, sin_ref, x32, qs, ks, vs, acc_ref, m_ref, l_ref,
                        S=S, dil=dil, first=(gi == 0))

    @pl.when(g == len(DILATIONS) - 1)
    def _():
        rc = 256

        def fin(c, carry):
            sl = pl.ds(pl.multiple_of(c * rc, rc), rc)
            o_ref[0, sl, :] = (acc_ref[sl, :] / l_ref[sl, :]).astype(o_ref.dtype)
            return carry

        lax.fori_loop(0, S // rc, fin, 0)


def _attention(proj3, cos_t, sin_t):
    B, S, _ = proj3.shape
    ng = len(DILATIONS)

    def col_spec(col0):
        base = col0 // LANES
        per_group = ATT_GROUP_WIDTH // LANES
        return pl.BlockSpec((1, S, LANES), lambda b, j, g: (b, 0, base + g * per_group + j))

    tab_spec = pl.BlockSpec((S, LANES), lambda b, j, g: (0, 0))
    seq_bf16 = pltpu.VMEM((S, LANES), BF16)
    seq_f32 = pltpu.VMEM((S, LANES), F32)
    return pl.pallas_call(
        functools.partial(_attn_kernel, S=S),
        grid=(B, ATT_GROUP_WIDTH // LANES, ng),
        in_specs=[col_spec(COL_ATT_Q), col_spec(COL_ATT_K), col_spec(COL_ATT_V), tab_spec, tab_spec],
        out_specs=pl.BlockSpec((1, S, LANES), lambda b, j, g: (b, 0, j)),
        out_shape=jax.ShapeDtypeStruct((B, S, ATT_GROUP_WIDTH), BF16),
        scratch_shapes=[pltpu.VMEM((3, S, LANES), F32),
                        pltpu.VMEM((2 * S, LANES), BF16), seq_bf16, seq_bf16, seq_f32, seq_f32, seq_f32],
        compiler_params=_params(("parallel", "parallel", "arbitrary"), 48),
    )(proj3, proj3, proj3, cos_t, sin_t)


def _rope_tables(S):
    half = ROPE_DIM // 2
    inv = ROPE_THETA ** (-jnp.arange(half, dtype=F32) / half)
    ang = jnp.arange(S, dtype=F32)[:, None] * inv[None, :]
    cos, sin = jnp.cos(ang), jnp.sin(ang)
    pad = ATT_HEAD_DIM - ROPE_DIM
    cos_h = jnp.concatenate([cos, cos, jnp.ones((S, pad), F32)], axis=1)
    sin_h = jnp.concatenate([sin, sin, jnp.zeros((S, pad), F32)], axis=1)
    tile = lambda t: jnp.concatenate([t, t], axis=1)
    return tile(cos_h), tile(sin_h)


DN_BLK = 128
DN_UNROLL = 4
DN_DOUBLINGS = 5


def _dn_kernel(q_ref, k_ref, v_ref, z_ref, sm_ref, cwq_ref, cwk_ref, cwv_ref, par_ref, ng_ref, o_ref,
               xp, qn, kn, vn, kq_s, b_s, dec_s, oacc, *, S):
    h = pl.program_id(1)
    rc = 256
    halo = 8

    xp[0:halo, :] = jnp.zeros((halo, LANES), F32)
    xp[S + halo:S + 2 * halo, :] = jnp.zeros((halo, LANES), F32)
    for src, cw, dst, mode in ((q_ref, cwq_ref, qn, "q"), (k_ref, cwk_ref, kn, "k"), (v_ref, cwv_ref, vn, "v")):
        def cp(c, carry, src=src):
            r0 = pl.multiple_of(c * rc, rc)
            xp[pl.ds(r0 + halo, rc), :] = src[0, pl.ds(r0, rc), :].astype(F32)
            return carry

        lax.fori_loop(0, S // rc, cp, 0)

        def cv(c, carry, cw=cw, dst=dst, mode=mode):
            r0 = pl.multiple_of(c * rc, rc)
            off = halo - SHORT_CONV // 2
            y = xp[pl.ds(r0 + off, rc), :] * cw[0:1, :]
            for i in range(1, SHORT_CONV):
                y = y + xp[pl.ds(r0 + off + i, rc), :] * cw[i:i + 1, :]
            y = y * jax.nn.sigmoid(y)
            if mode != "v":
                y = y * lax.rsqrt(jnp.sum(y * y, axis=-1, keepdims=True) + EPS)
            if mode == "q":
                y = y * (DN_HEAD_DIM ** -0.5)
            dst[pl.ds(r0, rc), :] = y
            return carry

        lax.fori_loop(0, S // rc, cv, 0, unroll=2)

    lane = lax.broadcasted_iota(jnp.int32, (DN_BLK, DN_BLK), 1)
    rowi = lax.broadcasted_iota(jnp.int32, (DN_BLK, DN_BLK), 0)
    eye = jnp.where(lane == rowi, 1.0, 0.0)
    same_half = (rowi >= DN_BLK // 2) == (lane >= DN_BLK // 2)
    a_exp = jnp.exp(par_ref[0:1, :])
    dt_bias = par_ref[1:2, :]
    n_blocks = S // DN_BLK

    def masks(d):
        return (lane <= rowi, lane < rowi) if d == 0 else (lane >= rowi, lane > rowi)

    tri3 = [jnp.concatenate([jnp.where(masks(d)[0], 1.0, 0.0).astype(BF16)] * 3, axis=1) for d in range(2)]

    def prepass(c, carry):
        chains = []
        for uu in range(DN_UNROLL):
            blk = c * DN_UNROLL + uu
            rows = pl.ds(pl.multiple_of(blk * DN_BLK, DN_BLK), DN_BLK)
            sm = sm_ref[0, rows, :]
            beta_all = jax.nn.sigmoid(sm)
            g_all = -a_exp * jax.nn.softplus(sm + dt_bias)
            q = qn[rows, :]
            k = kn[rows, :]
            v = vn[rows, :]
            for d in range(2):
                beta = jnp.sum(jnp.where(lane == h + DN_HEADS * d, beta_all, 0.0), axis=1, keepdims=True)
                g = jnp.sum(jnp.where(lane == h + DN_HEADS * (2 + d), g_all, 0.0), axis=1, keepdims=True)
                gb = jnp.broadcast_to(g, (DN_BLK, DN_BLK))
                hi = gb.astype(BF16)
                r1 = gb - hi.astype(F32)
                mid = r1.astype(BF16)
                lo = (r1 - mid.astype(F32)).astype(BF16)
                chains.append(dict(blk=blk, rows=rows, d=d, q=q, k=k, kb=k * beta, vb=v * beta,
                                   g3=jnp.concatenate([hi, mid, lo], axis=0)))
            fwd, bwd = chains[-2], chains[-1]
            kk = _mm_nt(jnp.concatenate([fwd["kb"], bwd["kb"], q], axis=0), k)
            fwd["kk"], bwd["kk"] = kk[:DN_BLK], kk[DN_BLK:2 * DN_BLK]
            fwd["qk"] = bwd["qk"] = kk[2 * DN_BLK:]
        for ch in chains:
            d = ch["d"]
            tri = masks(d)[0]
            gc = jnp.dot(tri3[d], ch["g3"], preferred_element_type=F32)
            last = DN_BLK - 1 if d == 0 else 0
            ch["gc"], ch["gl"] = gc, gc[last:last + 1, :]
            ch["dm"] = jnp.where(tri, jnp.exp(jnp.where(tri, gc - gc.T, 0.0)), 0.0)
        for ch in chains:
            nm = jnp.where(masks(ch["d"])[1], -(ch["kk"] * ch["dm"]), 0.0)
            ch["nd"] = jnp.where(same_half, nm, 0.0)
            ch["no"] = jnp.where(same_half, 0.0, nm)
        for ch in chains:
            ch["x"] = eye + ch["nd"]
            ch["p"] = _mm(ch["nd"], ch["nd"])
        for _ in range(DN_DOUBLINGS - 1):
            for ch in chains:
                r = _mm(jnp.concatenate([ch["x"], ch["p"]], axis=0), ch["p"])
                ch["x"] = ch["x"] + r[:DN_BLK]
                ch["p"] = r[DN_BLK:]
        for ch in chains:
            ch["x"] = ch["x"] + _mm(ch["x"], ch["p"])
            ch["y"] = _mm(ch["no"], ch["x"])
        for ch in chains:
            eg = jnp.exp(ch["gc"])
            x = ch["x"] + _mm(ch["x"], ch["y"])
            ch["wu"] = _mm(x, jnp.concatenate([ch["kb"] * eg, ch["vb"]], axis=1))
            ch["qg"] = ch["q"] * eg
            ch["kg"] = ch["k"] * jnp.exp(ch["gl"] - ch["gc"])
        for ch in chains:
            iw = _mm(ch["qk"] * ch["dm"], ch["wu"])
            ch["qp"] = ch["qg"] - iw[:, :LANES]
            ch["oc"] = iw[:, LANES:]
        for ch in chains:
            d, blk = ch["d"], ch["blk"]
            kb_ = _mm_tn(ch["kg"], ch["wu"])
            base = pl.multiple_of(blk * 2 * DN_BLK, 2 * DN_BLK)
            kq_s[d, pl.ds(base, DN_BLK), :] = (-kb_[:, :LANES]).astype(BF16)
            kq_s[d, pl.ds(base + DN_BLK, DN_BLK), :] = ch["qp"].astype(BF16)
            b_s[d, ch["rows"], :] = kb_[:, LANES:]
            dec_s[d, pl.ds(pl.multiple_of(blk * 8, 8), 8), :] = jnp.broadcast_to(jnp.exp(ch["gl"]), (8, LANES))
        for uu in range(DN_UNROLL):
            oacc[chains[2 * uu]["rows"], :] = chains[2 * uu]["oc"] + chains[2 * uu + 1]["oc"]
        return carry

    lax.fori_loop(0, n_blocks // DN_UNROLL, prepass, 0)

    def step(d, blk, state):
        base = pl.multiple_of(blk * 2 * DN_BLK, 2 * DN_BLK)
        r = jnp.dot(kq_s[d, pl.ds(base, 2 * DN_BLK), :], state.astype(BF16), preferred_element_type=F32)
        rows = pl.ds(pl.multiple_of(blk * DN_BLK, DN_BLK), DN_BLK)
        oacc[rows, :] += r[DN_BLK:]
        dec = dec_s[d, pl.ds(pl.multiple_of(blk * 8, 8), 1), :]
        return state * dec + r[:DN_BLK] + b_s[d, rows, :]

    def rec(t, carry):
        sf, sb = carry
        bf = 2 * t
        bb = n_blocks - 1 - 2 * t
        sf = step(0, bf, sf)
        sb = step(1, bb, sb)
        sf = step(0, bf + 1, sf)
        sb = step(1, bb - 1, sb)
        return sf, sb

    zero_state = jnp.zeros((DN_HEAD_DIM, DN_HEAD_DIM), F32)
    lax.fori_loop(0, n_blocks // 2, rec, (zero_state, zero_state))

    def fin(c, carry):
        sl = pl.ds(pl.multiple_of(c * rc, rc), rc)
        o = oacc[sl, :]
        o = o * lax.rsqrt(jnp.mean(o * o, axis=-1, keepdims=True) + EPS) * ng_ref[...]
        z = z_ref[0, sl, :].astype(F32)
        o_ref[0, sl, :] = (o * (z * jax.nn.sigmoid(z))).astype(o_ref.dtype)
        return carry

    lax.fori_loop(0, S // rc, fin, 0)


def _deltanet(proj3, small3, conv_w, par, norm_g):
    B, S, _ = proj3.shape

    def col_spec(col0):
        base = col0 // LANES
        return pl.BlockSpec((1, S, LANES), lambda b, h: (b, 0, base + h))

    def conv_spec(part):
        return pl.BlockSpec((SHORT_CONV, LANES), lambda b, h: (0, part * DN_HEADS + h))

    seq_f32 = pltpu.VMEM((S, LANES), F32)
    return pl.pallas_call(
        functools.partial(_dn_kernel, S=S),
        grid=(B, DN_HEADS),
        in_specs=[col_spec(COL_DN_Q), col_spec(COL_DN_K), col_spec(COL_DN_V), col_spec(COL_Z),
                  pl.BlockSpec((1, S, LANES), lambda b, h: (b, 0, 0)),
                  conv_spec(0), conv_spec(1), conv_spec(2),
                  pl.BlockSpec((8, LANES), lambda b, h: (0, 0)),
                  pl.BlockSpec((1, LANES), lambda b, h: (0, 0))],
        out_specs=pl.BlockSpec((1, S, LANES), lambda b, h: (b, 0, h)),
        out_shape=jax.ShapeDtypeStruct((B, S, DN_WIDTH), BF16),
        scratch_shapes=[pltpu.VMEM((S + 16, LANES), F32), seq_f32, seq_f32, seq_f32,
                        pltpu.VMEM((2, 2 * S, LANES), BF16),
                        pltpu.VMEM((2, S, LANES), F32),
                        pltpu.VMEM((2, S // DN_BLK * 8, LANES), F32),
                        seq_f32],
        compiler_params=_params(("parallel", "parallel"), 56),
    )(proj3, proj3, proj3, proj3, small3, conv_w, conv_w, conv_w, par, norm_g.reshape(1, LANES))


def _mix_kernel(x_ref, oa_ref, ob_ref, ga_ref, gb_ref, wa_ref, wb_ref, wo_ref, o_ref):
    ya = jnp.dot(oa_ref[...], wa_ref[...], preferred_element_type=F32)
    yb = jnp.dot(ob_ref[...], wb_ref[...], preferred_element_type=F32)
    mix = jax.nn.sigmoid(ga_ref[...].astype(F32)) * ya + jax.nn.sigmoid(gb_ref[...].astype(F32)) * yb
    o_ref[...] = x_ref[...] + jnp.dot(mix.astype(BF16), wo_ref[...], preferred_element_type=F32)


def _mix(x2d, o_att, o_dn, proj, wa, wb, wo, *, tm):
    T = x2d.shape[0]
    row = lambda width, cb=0: pl.BlockSpec((tm, width), lambda i: (i, cb))
    full = lambda a: pl.BlockSpec(a.shape, lambda i: (0, 0))
    return pl.pallas_call(
        _mix_kernel,
        grid=(T // tm,),
        in_specs=[row(D_MODEL), row(ATT_GROUP_WIDTH), row(DN_WIDTH),
                  row(D_MODEL, COL_GATE_A // D_MODEL), row(D_MODEL, COL_GATE_B // D_MODEL),
                  full(wa), full(wb), full(wo)],
        out_specs=row(D_MODEL),
        out_shape=jax.ShapeDtypeStruct((T, D_MODEL), F32),
        compiler_params=_params(("parallel",), 48),
    )(x2d, o_att, o_dn, proj, proj, wa, wb, wo)


FFN_KC = 256
FFN_HALO = 16


def _ffn_out_kernel(g_ref, v_ref, gp_ref, gn_ref, x_ref, cw_ref, cb_ref, wd_ref, nf_ref, o_ref, act_ref,
                    *, tiles_per_seq):
    i = pl.program_id(0)
    tm = g_ref.shape[0]
    pos = i % tiles_per_seq
    keep_prev = jnp.where(pos == 0, 0.0, 1.0)
    keep_next = jnp.where(pos == tiles_per_seq - 1, 0.0, 1.0)
    row = lax.broadcasted_iota(jnp.int32, (tm, FFN_KC), 0)
    for c in range(D_FF // FFN_KC):
        sl = slice(c * FFN_KC, (c + 1) * FFN_KC)
        g = g_ref[:, sl].astype(F32)
        prev_row = gp_ref[FFN_HALO - 1:FFN_HALO, sl].astype(F32) * keep_prev
        next_row = gn_ref[0:1, sl].astype(F32) * keep_next
        g_prev = jnp.where(row == 0, prev_row, pltpu.roll(g, 1, 0))
        g_next = jnp.where(row == tm - 1, next_row, pltpu.roll(g, tm - 1, 0))
        conv = g_prev * cw_ref[0:1, sl] + g * cw_ref[1:2, sl] + g_next * cw_ref[2:3, sl] + cb_ref[:, sl]
        gelu = 0.5 * conv * (1.0 + lax.erf(conv * (2.0 ** -0.5)))
        act_ref[:, sl] = (gelu * v_ref[:, sl].astype(F32)).astype(BF16)
    x2 = x_ref[...] + jnp.dot(act_ref[...], wd_ref[...], preferred_element_type=F32)
    ms = jnp.mean(x2 * x2, axis=-1, keepdims=True)
    o_ref[...] = x2 * lax.rsqrt(ms + EPS) * nf_ref[...]


def _ffn_out(up, x1, conv_w, conv_b, wd, norm_g, *, tm, S):
    T = x1.shape[0]
    tiles_per_seq = S // tm
    hb = tm // FFN_HALO
    last_hb = T // FFN_HALO - 1
    return pl.pallas_call(
        functools.partial(_ffn_out_kernel, tiles_per_seq=tiles_per_seq),
        grid=(T // tm,),
        in_specs=[
            pl.BlockSpec((tm, D_FF), lambda i: (i, 0)),
            pl.BlockSpec((tm, D_FF), lambda i: (i, 1)),
            pl.BlockSpec((FFN_HALO, D_FF), lambda i: (jnp.maximum(i * hb - 1, 0), 0)),
            pl.BlockSpec((FFN_HALO, D_FF), lambda i: (jnp.minimum((i + 1) * hb, last_hb), 0)),
            pl.BlockSpec((tm, D_MODEL), lambda i: (i, 0)),
            pl.BlockSpec((3, D_FF), lambda i: (0, 0)),
            pl.BlockSpec((1, D_FF), lambda i: (0, 0)),
            pl.BlockSpec((D_FF, D_MODEL), lambda i: (0, 0)),
            pl.BlockSpec((1, D_MODEL), lambda i: (0, 0)),
        ],
        out_specs=pl.BlockSpec((tm, D_MODEL), lambda i: (i, 0)),
        out_shape=jax.ShapeDtypeStruct((T, D_MODEL), F32),
        scratch_shapes=[pltpu.VMEM((tm, D_FF), BF16)],
        compiler_params=_params(("parallel",), 56),
    )(up, up, up, up, x1, conv_w, conv_b.reshape(1, D_FF), wd, norm_g.reshape(1, D_MODEL))


def _trunk(x, w):
    B, S, _ = x.shape
    T = B * S
    x2d = x.reshape(T, D_MODEL)
    proj, small = _norm_matmul(x2d, w["norm_mix_g"], w["w_in"], w["w_small"], tm=1024, tn=1536)
    proj3 = proj.reshape(B, S, PROJ_COLS)
    o_att = _attention(proj3, *_rope_tables(S))
    o_dn = _deltanet(proj3, small.reshape(B, S, LANES), w["conv_qkv_w"], w["dn_par"], w["out_norm_g"])
    x1 = _mix(x2d, o_att.reshape(T, ATT_GROUP_WIDTH), o_dn.reshape(T, DN_WIDTH), proj,
              w["w_branch_a"], w["w_branch_b"], w["w_out"], tm=512)
    up = _norm_matmul(x1, w["norm_ffn_g"], w["w_up"], tm=1024, tn=D_FF)[0]
    y = _ffn_out(up, x1, w["ffn_conv_w"], w["ffn_conv_b"], w["w_down"], w["norm_final_g"], tm=512, S=S)
    return y.reshape(B, S, D_MODEL)


def kernel(x_prompt, x_sample, norm_mix_g, w_in, conv_qkv_w, a_log_f, a_log_b, dt_bias_f, dt_bias_b, out_norm_g, w_branch_a, w_branch_b, w_out, norm_ffn_g, w_up, ffn_conv_w, ffn_conv_b, w_down, norm_final_g):
    att = 3 * ATT_WIDTH
    dn_end = att + 3 * DN_WIDTH
    z_end = dn_end + DN_WIDTH
    small_end = z_end + 4 * DN_HEADS
    w_in_r = jnp.concatenate([
        w_in[:, small_end:],
        w_in[:, dn_end:z_end],
        w_in[:, att:dn_end],
        w_in[:, :ATT_WIDTH] * (ATT_HEAD_DIM ** -0.5),
        w_in[:, ATT_WIDTH:att],
    ], axis=1).astype(BF16)
    w_small = jnp.pad(w_in[:, z_end:small_end], ((0, 0), (0, LANES - 4 * DN_HEADS))).astype(BF16)
    par = jnp.zeros((8, LANES), F32)
    par = par.at[0, 2 * DN_HEADS:4 * DN_HEADS].set(jnp.concatenate([a_log_f, a_log_b]))
    par = par.at[1, 2 * DN_HEADS:4 * DN_HEADS].set(jnp.concatenate([dt_bias_f, dt_bias_b]))
    w = dict(
        norm_mix_g=norm_mix_g, w_in=w_in_r, w_small=w_small, conv_qkv_w=conv_qkv_w, dn_par=par, out_norm_g=out_norm_g,
        w_branch_a=w_branch_a.astype(BF16), w_branch_b=w_branch_b.astype(BF16), w_out=w_out.astype(BF16),
        norm_ffn_g=norm_ffn_g, w_up=w_up.astype(BF16), ffn_conv_w=ffn_conv_w, ffn_conv_b=ffn_conv_b,
        w_down=w_down.astype(BF16), norm_final_g=norm_final_g,
    )
    return _trunk(x_prompt, w), _trunk(x_sample, w)
```

```python
import functools

import jax
import jax.numpy as jnp
from jax import lax
from jax.experimental import pallas as pl
from jax.experimental.pallas import tpu as pltpu

F32 = jnp.float32
BF16 = jnp.bfloat16

D_MODEL = 1024
ATT_HEAD_DIM = 64
ATT_HEADS_PER_GROUP = 8
DILATIONS = (1, 4, 16)
ATT_HALF = 64
ATT_GROUP_WIDTH = ATT_HEADS_PER_GROUP * ATT_HEAD_DIM
ATT_WIDTH = len(DILATIONS) * ATT_GROUP_WIDTH
ROPE_DIM = ATT_HEAD_DIM // 4
ROPE_THETA = 500000.0
DN_HEADS = 8
DN_HEAD_DIM = 128
DN_WIDTH = DN_HEADS * DN_HEAD_DIM
SHORT_CONV = 5
D_FF = 2816
EPS = 1e-6
NEG_INF = -1e30
LOG2_E = 1.4426950408889634

LANES = 128
MIB = 1024 * 1024

COL_GATE_A = 0
COL_GATE_B = 1024
COL_Z = 2048
COL_DN_Q = 3072
COL_DN_K = 4096
COL_DN_V = 5120
COL_ATT_Q = 6144
COL_ATT_K = COL_ATT_Q + ATT_WIDTH
COL_ATT_V = COL_ATT_K + ATT_WIDTH
PROJ_COLS = COL_ATT_V + ATT_WIDTH


def _params(semantics, vmem_mib):
    return pltpu.CompilerParams(dimension_semantics=semantics, vmem_limit_bytes=vmem_mib * MIB)


def _mm(a, b):
    return jnp.dot(a.astype(BF16), b.astype(BF16), preferred_element_type=F32)


def _mm_nt(a, b):
    return lax.dot_general(a.astype(BF16), b.astype(BF16), (((1,), (1,)), ((), ())),
                           preferred_element_type=F32)


def _mm_tn(a, b):
    return lax.dot_general(a.astype(BF16), b.astype(BF16), (((0,), (0,)), ((), ())),
                           preferred_element_type=F32)


def _norm_matmul_kernel(*refs, row_chunk, with_side):
    if with_side:
        x_ref, g_ref, w_ref, ws_ref, o_ref, os_ref, h_ref = refs
    else:
        x_ref, g_ref, w_ref, o_ref, h_ref = refs

    @pl.when(pl.program_id(1) == 0)
    def _():
        def body(c, carry):
            r0 = pl.multiple_of(c * row_chunk, row_chunk)
            x = x_ref[pl.ds(r0, row_chunk), :]
            ms = jnp.mean(x * x, axis=-1, keepdims=True)
            h_ref[pl.ds(r0, row_chunk), :] = (x * lax.rsqrt(ms + EPS) * g_ref[...]).astype(BF16)
            return carry
        lax.fori_loop(0, x_ref.shape[0] // row_chunk, body, 0)
        if with_side:
            os_ref[...] = jnp.dot(h_ref[...], ws_ref[...], preferred_element_type=F32)

    o_ref[...] = jnp.dot(h_ref[...], w_ref[...], preferred_element_type=F32).astype(o_ref.dtype)


def _norm_matmul(x2d, gain, w_bf16, w_side=None, *, tm, tn):
    T, K = x2d.shape
    N = w_bf16.shape[1]
    with_side = w_side is not None
    in_specs = [
        pl.BlockSpec((tm, K), lambda i, j: (i, 0)),
        pl.BlockSpec((1, K), lambda i, j: (0, 0)),
        pl.BlockSpec((K, tn), lambda i, j: (0, j)),
    ]
    out_specs = [pl.BlockSpec((tm, tn), lambda i, j: (i, j))]
    out_shape = [jax.ShapeDtypeStruct((T, N), BF16)]
    args = [x2d, gain.reshape(1, K), w_bf16]
    if with_side:
        ns = w_side.shape[1]
        in_specs.append(pl.BlockSpec((K, ns), lambda i, j: (0, 0)))
        out_specs.append(pl.BlockSpec((tm, ns), lambda i, j: (i, 0)))
        out_shape.append(jax.ShapeDtypeStruct((T, ns), F32))
        args.append(w_side)
    return pl.pallas_call(
        functools.partial(_norm_matmul_kernel, row_chunk=128, with_side=with_side),
        grid=(T // tm, N // tn),
        in_specs=in_specs,
        out_specs=out_specs,
        out_shape=out_shape,
        scratch_shapes=[pltpu.VMEM((tm, K), BF16)],
        compiler_params=_params(("parallel", "arbitrary"), 48),
    )(*args)


ATT_QB = 128
ATT_UNROLL = 4


def _attn_group(q_ref, k_ref, v_ref, cos_ref, sin_ref, x32, bias_ref, qs, ks, vs, acc_ref, m_ref, l_ref,
                *, S, dil, first):
    L = S // dil
    rc = min(L, 256)
    nlc = L // rc

    def rows_of(start, n):
        return pl.ds(start, n, stride=dil) if dil > 1 else pl.ds(pl.multiple_of(start, ATT_QB), n)

    if dil > 1:
        def widen(c, carry):
            sl = pl.ds(pl.multiple_of(c * 256, 256), 256)
            for a, ref in enumerate((q_ref, k_ref, v_ref)):
                x32[a, sl, :] = ref[0, sl, :].astype(F32)
            return carry

        lax.fori_loop(0, S // 256, widen, 0)
        load = lambda a, rows: x32[a, rows, :]
    else:
        load = lambda a, rows: (q_ref, k_ref, v_ref)[a][0, rows, :].astype(F32)

    lane_in = lax.broadcasted_iota(jnp.int32, (LANES, LANES), 0)
    lane_out = lax.broadcasted_iota(jnp.int32, (LANES, LANES), 1)
    head_lane = lane_out % ATT_HEAD_DIM
    half = ROPE_DIM // 2
    rot_m = jnp.where((head_lane < half) & (lane_in == lane_out + half), -1.0,
                      jnp.where((head_lane >= half) & (head_lane < ROPE_DIM) & (lane_in == lane_out - half),
                                1.0, 0.0)).astype(BF16)

    def pre(idx, carry):
        r = idx // nlc if dil > 1 else 0
        lc = idx % nlc
        rows = rows_of(r + dil * lc * rc, rc)
        pos = r * L + lc * rc
        cs = cos_ref[rows, :]
        sn = sin_ref[rows, :]

        def rope(t):
            return t * cs + jnp.dot(t.astype(BF16), rot_m, preferred_element_type=F32) * sn

        q = rope(load(0, rows)) * LOG2_E
        ks[pl.ds(pl.multiple_of(pos, rc), rc), :] = rope(load(1, rows)).astype(BF16)
        vs[pl.ds(pl.multiple_of(pos, rc), rc), :] = load(2, rows).astype(BF16)
        head_a = lax.broadcasted_iota(jnp.int32, (rc, LANES), 1) < ATT_HEAD_DIM
        qa = jnp.where(head_a, q, 0.0).astype(BF16)
        qb = jnp.where(head_a, 0.0, q).astype(BF16)
        for sub in range(rc // ATT_QB):
            dst = pl.multiple_of(2 * pos + 2 * sub * ATT_QB, 2 * ATT_QB)
            qs[pl.ds(dst, ATT_QB), :] = qa[sub * ATT_QB:(sub + 1) * ATT_QB]
            qs[pl.ds(dst + ATT_QB, ATT_QB), :] = qb[sub * ATT_QB:(sub + 1) * ATT_QB]
        return carry

    lax.fori_loop(0, dil * nlc, pre, 0, unroll=2)

    nb = L // ATT_QB
    nk = min(2 * ATT_QB, L)
    col = lax.broadcasted_iota(jnp.int32, (ATT_QB, nk), 1)
    row = lax.broadcasted_iota(jnp.int32, (ATT_QB, nk), 0)
    head_a = lax.broadcasted_iota(jnp.int32, (ATT_QB, LANES), 1) < ATT_HEAD_DIM
    for v in range(3):
        band = jnp.where(jnp.abs(col - row - v * ATT_HALF) <= ATT_HALF, 0.0, NEG_INF)
        bias_ref[v, 0:ATT_QB, 0:nk] = band
        bias_ref[v, ATT_QB:2 * ATT_QB, 0:nk] = band

    def both(t):
        return jnp.where(head_a, t[:ATT_QB], t[ATT_QB:])

    def blk(it, carry):
        work = []
        for ub in range(ATT_UNROLL):
            idx = it * ATT_UNROLL + ub
            r = idx // nb if dil > 1 else 0
            n = idx % nb
            q0 = n * ATT_QB
            start = jnp.clip(q0 - ATT_HALF, 0, L - nk)
            qsl = pl.ds(pl.multiple_of(2 * (r * L + q0), 2 * ATT_QB), 2 * ATT_QB)
            ksl = pl.ds(pl.multiple_of(r * L + start, ATT_HALF), nk)
            s = lax.dot_general(qs[qsl, :], ks[ksl, :], (((1,), (1,)), ((), ())), preferred_element_type=F32)
            work.append(dict(rows=rows_of(r + dil * q0, ATT_QB), ksl=ksl, s=s,
                             variant=(q0 - start) // ATT_HALF))
        for w in work:
            s = w["s"] + bias_ref[w["variant"], :, 0:nk]
            w["m"] = jnp.max(s, axis=1, keepdims=True)
            p = jnp.exp2(s - w["m"])
            w["l"] = jnp.sum(p, axis=1, keepdims=True)
            w["p"] = p.astype(BF16)
        for w in work:
            w["pv"] = jnp.dot(w["p"], vs[w["ksl"], :], preferred_element_type=F32)
        for w in work:
            rows, m, l, pv = w["rows"], both(w["m"]), both(w["l"]), both(w["pv"])
            if first:
                m_ref[rows, :] = m
                l_ref[rows, :] = l
                acc_ref[rows, :] = pv
            else:
                m_old = m_ref[rows, :]
                m_new = jnp.maximum(m_old, m)
                a = jnp.exp2(m_old - m_new)
                b = jnp.exp2(m - m_new)
                m_ref[rows, :] = m_new
                l_ref[rows, :] = a * l_ref[rows, :] + b * l
                acc_ref[rows, :] = a * acc_ref[rows, :] + b * pv
        return carry

    lax.fori_loop(0, dil * nb // ATT_UNROLL, blk, 0)


def _attn_kernel(q_ref, k_ref, v_ref, cos_ref, sin_ref, o_ref,
                 x32, bias_ref, qs, ks, vs, acc_ref, m_ref, l_ref, *, S):
    g = pl.program_id(2)
    for gi, dil in enumerate(DILATIONS):
        @pl.when(g == gi)
        def _(gi=gi, dil=dil):
            _attn_group(q_ref, k_ref, v_ref, cos_ref, sin_ref, x32, bias_ref, qs, ks, vs, acc_ref, m_ref, l_ref,
                        S=S, dil=dil, first=(gi == 0))

    @pl.when(g == len(DILATIONS) - 1)
    def _():
        rc = 256

        def fin(c, carry):
            sl = pl.ds(pl.multiple_of(c * rc, rc), rc)
            o_ref[0, sl, :] = (acc_ref[sl, :] / l_ref[sl, :]).astype(o_ref.dtype)
            return carry

        lax.fori_loop(0, S // rc, fin, 0)


def _attention(proj3, cos_t, sin_t):
    B, S, _ = proj3.shape
    ng = len(DILATIONS)

    def col_spec(col0):
        base = col0 // LANES
        per_group = ATT_GROUP_WIDTH // LANES
        return pl.BlockSpec((1, S, LANES), lambda b, j, g: (b, 0, base + g * per_group + j))

    tab_spec = pl.BlockSpec((S, LANES), lambda b, j, g: (0, 0))
    seq_bf16 = pltpu.VMEM((S, LANES), BF16)
    seq_f32 = pltpu.VMEM((S, LANES), F32)
    return pl.pallas_call(
        functools.partial(_attn_kernel, S=S),
        grid=(B, ATT_GROUP_WIDTH // LANES, ng),
        in_specs=[col_spec(COL_ATT_Q), col_spec(COL_ATT_K), col_spec(COL_ATT_V), tab_spec, tab_spec],
        out_specs=pl.BlockSpec((1, S, LANES), lambda b, j, g: (b, 0, j)),
        out_shape=jax.ShapeDtypeStruct((B, S, ATT_GROUP_WIDTH), BF16),
        scratch_shapes=[pltpu.VMEM((3, S, LANES), F32),
                        pltpu.VMEM((3, 2 * ATT_QB, 2 * ATT_QB), F32),
                        pltpu.VMEM((2 * S, LANES), BF16), seq_bf16, seq_bf16, seq_f32, seq_f32, seq_f32],
        compiler_params=_params(("parallel", "parallel", "arbitrary"), 48),
    )(proj3, proj3, proj3, cos_t, sin_t)


def _rope_tables(S):
    half = ROPE_DIM // 2
    inv = ROPE_THETA ** (-jnp.arange(half, dtype=F32) / half)
    ang = jnp.arange(S, dtype=F32)[:, None] * inv[None, :]
    cos, sin = jnp.cos(ang), jnp.sin(ang)
    pad = ATT_HEAD_DIM - ROPE_DIM
    cos_h = jnp.concatenate([cos, cos, jnp.ones((S, pad), F32)], axis=1)
    sin_h = jnp.concatenate([sin, sin, jnp.zeros((S, pad), F32)], axis=1)
    tile = lambda t: jnp.concatenate([t, t], axis=1)
    return tile(cos_h), tile(sin_h)


DN_BLK = 128
DN_UNROLL = 8
DN_DOUBLINGS = 5


def _dn_kernel(q_ref, k_ref, v_ref, z_ref, sm_ref, cwq_ref, cwk_ref, cwv_ref, par_ref, ng_ref, o_ref,
               xp, qn, kn, vn, kq_s, b_s, dec_s, oacc, *, S):
    h = pl.program_id(1)
    rc = 256
    halo = 8

    xp[0:halo, :] = jnp.zeros((halo, LANES), F32)
    xp[S + halo:S + 2 * halo, :] = jnp.zeros((halo, LANES), F32)
    for src, cw, dst, mode in ((q_ref, cwq_ref, qn, "q"), (k_ref, cwk_ref, kn, "k"), (v_ref, cwv_ref, vn, "v")):
        def cp(c, carry, src=src):
            r0 = pl.multiple_of(c * rc, rc)
            xp[pl.ds(r0 + halo, rc), :] = src[0, pl.ds(r0, rc), :].astype(F32)
            return carry

        lax.fori_loop(0, S // rc, cp, 0)

        def cv(c, carry, cw=cw, dst=dst, mode=mode):
            r0 = pl.multiple_of(c * rc, rc)
            off = halo - SHORT_CONV // 2
            y = xp[pl.ds(r0 + off, rc), :] * cw[0:1, :]
            for i in range(1, SHORT_CONV):
                y = y + xp[pl.ds(r0 + off + i, rc), :] * cw[i:i + 1, :]
            y = y * jax.nn.sigmoid(y)
            if mode != "v":
                y = y * lax.rsqrt(jnp.sum(y * y, axis=-1, keepdims=True) + EPS)
            if mode == "q":
                y = y * (DN_HEAD_DIM ** -0.5)
            dst[pl.ds(r0, rc), :] = y
            return carry

        lax.fori_loop(0, S // rc, cv, 0, unroll=2)

    lane = lax.broadcasted_iota(jnp.int32, (DN_BLK, DN_BLK), 1)
    rowi = lax.broadcasted_iota(jnp.int32, (DN_BLK, DN_BLK), 0)
    eye = jnp.where(lane == rowi, 1.0, 0.0)
    same_half = (rowi >= DN_BLK // 2) == (lane >= DN_BLK // 2)
    a_exp = jnp.exp(par_ref[0:1, :])
    dt_bias = par_ref[1:2, :]
    n_blocks = S // DN_BLK

    def masks(d):
        return (lane <= rowi, lane < rowi) if d == 0 else (lane >= rowi, lane > rowi)

    tri3 = [jnp.concatenate([jnp.where(masks(d)[0], 1.0, 0.0).astype(BF16)] * 3, axis=1) for d in range(2)]

    def prepass(c, carry):
        chains = []
        for uu in range(DN_UNROLL):
            blk = c * DN_UNROLL + uu
            rows = pl.ds(pl.multiple_of(blk * DN_BLK, DN_BLK), DN_BLK)
            sm = sm_ref[0, rows, :]
            beta_all = jax.nn.sigmoid(sm)
            g_all = -a_exp * jax.nn.softplus(sm + dt_bias)
            q = qn[rows, :]
            k = kn[rows, :]
            v = vn[rows, :]
            for d in range(2):
                beta = jnp.sum(jnp.where(lane == h + DN_HEADS * d, beta_all, 0.0), axis=1, keepdims=True)
                g = jnp.sum(jnp.where(lane == h + DN_HEADS * (2 + d), g_all, 0.0), axis=1, keepdims=True)
                gb = jnp.broadcast_to(g, (DN_BLK, DN_BLK))
                hi = gb.astype(BF16)
                r1 = gb - hi.astype(F32)
                mid = r1.astype(BF16)
                lo = (r1 - mid.astype(F32)).astype(BF16)
                chains.append(dict(blk=blk, rows=rows, d=d, q=q, k=k, kb=k * beta, vb=v * beta,
                                   g3=jnp.concatenate([hi, mid, lo], axis=0)))
            fwd, bwd = chains[-2], chains[-1]
            kk = _mm_nt(jnp.concatenate([fwd["kb"], bwd["kb"], q], axis=0), k)
            fwd["kk"], bwd["kk"] = kk[:DN_BLK], kk[DN_BLK:2 * DN_BLK]
            fwd["qk"] = bwd["qk"] = kk[2 * DN_BLK:]
        for ch in chains:
            d = ch["d"]
            tri = masks(d)[0]
            gc = jnp.dot(tri3[d], ch["g3"], preferred_element_type=F32)
            last = DN_BLK - 1 if d == 0 else 0
            ch["gc"], ch["gl"] = gc, gc[last:last + 1, :]
            ch["dm"] = jnp.where(tri, jnp.exp(jnp.where(tri, gc - gc.T, 0.0)), 0.0)
        for ch in chains:
            nm = jnp.where(masks(ch["d"])[1], -(ch["kk"] * ch["dm"]), 0.0)
            ch["nd"] = jnp.where(same_half, nm, 0.0)
            ch["no"] = jnp.where(same_half, 0.0, nm)
        for ch in chains:
            ch["x"] = eye + ch["nd"]
            ch["p"] = _mm(ch["nd"], ch["nd"])
        for _ in range(DN_DOUBLINGS - 1):
            for ch in chains:
                r = _mm(jnp.concatenate([ch["x"], ch["p"]], axis=0), ch["p"])
                ch["x"] = ch["x"] + r[:DN_BLK]
                ch["p"] = r[DN_BLK:]
        for ch in chains:
            ch["x"] = ch["x"] + _mm(ch["x"], ch["p"])
            ch["y"] = _mm(ch["no"], ch["x"])
        for ch in chains:
            eg = jnp.exp(ch["gc"])
            x = ch["x"] + _mm(ch["x"], ch["y"])
            ch["wu"] = _mm(x, jnp.concatenate([ch["kb"] * eg, ch["vb"]], axis=1))
            ch["qg"] = ch["q"] * eg
            ch["kg"] = ch["k"] * jnp.exp(ch["gl"] - ch["gc"])
        for ch in chains:
            iw = _mm(ch["qk"] * ch["dm"], ch["wu"])
            ch["qp"] = ch["qg"] - iw[:, :LANES]
            ch["oc"] = iw[:, LANES:]
        for ch in chains:
            d, blk = ch["d"], ch["blk"]
            kb_ = _mm_tn(ch["kg"], ch["wu"])
            base = pl.multiple_of(blk * 2 * DN_BLK, 2 * DN_BLK)
            kq_s[d, pl.ds(base, DN_BLK), :] = (-kb_[:, :LANES]).astype(BF16)
            kq_s[d, pl.ds(base + DN_BLK, DN_BLK), :] = ch["qp"].astype(BF16)
            b_s[d, ch["rows"], :] = kb_[:, LANES:]
            dec_s[d, pl.ds(pl.multiple_of(blk * 8, 8), 8), :] = jnp.broadcast_to(jnp.exp(ch["gl"]), (8, LANES))
        for uu in range(DN_UNROLL):
            oacc[chains[2 * uu]["rows"], :] = chains[2 * uu]["oc"] + chains[2 * uu + 1]["oc"]
        return carry

    lax.fori_loop(0, n_blocks // DN_UNROLL, prepass, 0)

    def step(d, blk, state):
        base = pl.multiple_of(blk * 2 * DN_BLK, 2 * DN_BLK)
        r = jnp.dot(kq_s[d, pl.ds(base, 2 * DN_BLK), :], state.astype(BF16), preferred_element_type=F32)
        rows = pl.ds(pl.multiple_of(blk * DN_BLK, DN_BLK), DN_BLK)
        oacc[rows, :] += r[DN_BLK:]
        dec = dec_s[d, pl.ds(pl.multiple_of(blk * 8, 8), 1), :]
        return state * dec + r[:DN_BLK] + b_s[d, rows, :]

    def rec(t, carry):
        sf, sb = carry
        bf = 2 * t
        bb = n_blocks - 1 - 2 * t
        sf = step(0, bf, sf)
        sb = step(1, bb, sb)
        sf = step(0, bf + 1, sf)
        sb = step(1, bb - 1, sb)
        return sf, sb

    zero_state = jnp.zeros((DN_HEAD_DIM, DN_HEAD_DIM), F32)
    lax.fori_loop(0, n_blocks // 2, rec, (zero_state, zero_state))

    def fin(c, carry):
        sl = pl.ds(pl.multiple_of(c * rc, rc), rc)
        o = oacc[sl, :]
        o = o * lax.rsqrt(jnp.mean(o * o, axis=-1, keepdims=True) + EPS) * ng_ref[...]
        z = z_ref[0, sl, :].astype(F32)
        o_ref[0, sl, :] = (o * (z * jax.nn.sigmoid(z))).astype(o_ref.dtype)
        return carry

    lax.fori_loop(0, S // rc, fin, 0)


def _deltanet(proj3, small3, conv_w, par, norm_g):
    B, S, _ = proj3.shape

    def col_spec(col0):
        base = col0 // LANES
        return pl.BlockSpec((1, S, LANES), lambda b, h: (b, 0, base + h))

    def conv_spec(part):
        return pl.BlockSpec((SHORT_CONV, LANES), lambda b, h: (0, part * DN_HEADS + h))

    seq_f32 = pltpu.VMEM((S, LANES), F32)
    return pl.pallas_call(
        functools.partial(_dn_kernel, S=S),
        grid=(B, DN_HEADS),
        in_specs=[col_spec(COL_DN_Q), col_spec(COL_DN_K), col_spec(COL_DN_V), col_spec(COL_Z),
                  pl.BlockSpec((1, S, LANES), lambda b, h: (b, 0, 0)),
                  conv_spec(0), conv_spec(1), conv_spec(2),
                  pl.BlockSpec((8, LANES), lambda b, h: (0, 0)),
                  pl.BlockSpec((1, LANES), lambda b, h: (0, 0))],
        out_specs=pl.BlockSpec((1, S, LANES), lambda b, h: (b, 0, h)),
        out_shape=jax.ShapeDtypeStruct((B, S, DN_WIDTH), BF16),
        scratch_shapes=[pltpu.VMEM((S + 16, LANES), F32), seq_f32, seq_f32, seq_f32,
                        pltpu.VMEM((2, 2 * S, LANES), BF16),
                        pltpu.VMEM((2, S, LANES), F32),
                        pltpu.VMEM((2, S // DN_BLK * 8, LANES), F32),
                        seq_f32],
        compiler_params=_params(("parallel", "parallel"), 56),
    )(proj3, proj3, proj3, proj3, small3, conv_w, conv_w, conv_w, par, norm_g.reshape(1, LANES))


def _mix_kernel(x_ref, oa_ref, ob_ref, ga_ref, gb_ref, wa_ref, wb_ref, wo_ref, o_ref):
    ya = jnp.dot(oa_ref[...], wa_ref[...], preferred_element_type=F32)
    yb = jnp.dot(ob_ref[...], wb_ref[...], preferred_element_type=F32)
    mix = jax.nn.sigmoid(ga_ref[...].astype(F32)) * ya + jax.nn.sigmoid(gb_ref[...].astype(F32)) * yb
    o_ref[...] = x_ref[...] + jnp.dot(mix.astype(BF16), wo_ref[...], preferred_element_type=F32)


def _mix(x2d, o_att, o_dn, proj, wa, wb, wo, *, tm):
    T = x2d.shape[0]
    row = lambda width, cb=0: pl.BlockSpec((tm, width), lambda i: (i, cb))
    full = lambda a: pl.BlockSpec(a.shape, lambda i: (0, 0))
    return pl.pallas_call(
        _mix_kernel,
        grid=(T // tm,),
        in_specs=[row(D_MODEL), row(ATT_GROUP_WIDTH), row(DN_WIDTH),
                  row(D_MODEL, COL_GATE_A // D_MODEL), row(D_MODEL, COL_GATE_B // D_MODEL),
                  full(wa), full(wb), full(wo)],
        out_specs=row(D_MODEL),
        out_shape=jax.ShapeDtypeStruct((T, D_MODEL), F32),
        compiler_params=_params(("parallel",), 48),
    )(x2d, o_att, o_dn, proj, proj, wa, wb, wo)


FFN_KC = 256
FFN_ROWS = 256
FFN_HALO = 16


def _ffn_out_kernel(g_ref, v_ref, gp_ref, gn_ref, x_ref, cw_ref, cb_ref, wd_ref, nf_ref, o_ref, act_ref,
                    *, tiles_per_seq):
    i = pl.program_id(0)
    tm = g_ref.shape[0]
    pos = i % tiles_per_seq
    keep_prev = jnp.where(pos == 0, 0.0, 1.0)
    keep_next = jnp.where(pos == tiles_per_seq - 1, 0.0, 1.0)
    n_slices = tm // FFN_ROWS
    ext_rows = FFN_ROWS + 2 * FFN_HALO
    for r in range(n_slices):
        r0 = r * FFN_ROWS
        rows = slice(r0, r0 + FFN_ROWS)
        for c in range(D_FF // FFN_KC):
            sl = slice(c * FFN_KC, (c + 1) * FFN_KC)
            if r == 0:
                above = gp_ref[:, sl].astype(F32) * keep_prev
            else:
                above = g_ref[r0 - FFN_HALO:r0, sl].astype(F32)
            if r == n_slices - 1:
                below = gn_ref[:, sl].astype(F32) * keep_next
            else:
                below = g_ref[r0 + FFN_ROWS:r0 + FFN_ROWS + FFN_HALO, sl].astype(F32)
            ext = jnp.concatenate([above, g_ref[rows, sl].astype(F32), below], axis=0)
            inner = slice(FFN_HALO, FFN_HALO + FFN_ROWS)
            g_prev = pltpu.roll(ext, 1, 0)[inner]
            g_next = pltpu.roll(ext, ext_rows - 1, 0)[inner]
            conv = (g_prev * cw_ref[0:1, sl] + ext[inner] * cw_ref[1:2, sl] + g_next * cw_ref[2:3, sl]
                    + cb_ref[:, sl])
            gelu = 0.5 * conv * (1.0 + lax.erf(conv * (2.0 ** -0.5)))
            act_ref[rows, sl] = (gelu * v_ref[rows, sl].astype(F32)).astype(BF16)
        x2 = x_ref[rows, :] + jnp.dot(act_ref[rows, :], wd_ref[...], preferred_element_type=F32)
        ms = jnp.mean(x2 * x2, axis=-1, keepdims=True)
        o_ref[rows, :] = x2 * lax.rsqrt(ms + EPS) * nf_ref[...]


def _ffn_out(up, x1, conv_w, conv_b, wd, norm_g, *, tm, S):
    T = x1.shape[0]
    tiles_per_seq = S // tm
    hb = tm // FFN_HALO
    last_hb = T // FFN_HALO - 1
    return pl.pallas_call(
        functools.partial(_ffn_out_kernel, tiles_per_seq=tiles_per_seq),
        grid=(T // tm,),
        in_specs=[
            pl.BlockSpec((tm, D_FF), lambda i: (i, 0)),
            pl.BlockSpec((tm, D_FF), lambda i: (i, 1)),
            pl.BlockSpec((FFN_HALO, D_FF), lambda i: (jnp.maximum(i * hb - 1, 0), 0)),
            pl.BlockSpec((FFN_HALO, D_FF), lambda i: (jnp.minimum((i + 1) * hb, last_hb), 0)),
            pl.BlockSpec((tm, D_MODEL), lambda i: (i, 0)),
            pl.BlockSpec((3, D_FF), lambda i: (0, 0)),
            pl.BlockSpec((1, D_FF), lambda i: (0, 0)),
            pl.BlockSpec((D_FF, D_MODEL), lambda i: (0, 0)),
            pl.BlockSpec((1, D_MODEL), lambda i: (0, 0)),
        ],
        out_specs=pl.BlockSpec((tm, D_MODEL), lambda i: (i, 0)),
        out_shape=jax.ShapeDtypeStruct((T, D_MODEL), F32),
        scratch_shapes=[pltpu.VMEM((tm, D_FF), BF16)],
        compiler_params=_params(("parallel",), 56),
    )(up, up, up, up, x1, conv_w, conv_b.reshape(1, D_FF), wd, norm_g.reshape(1, D_MODEL))


def _trunk(x, w):
    B, S, _ = x.shape
    T = B * S
    x2d = x.reshape(T, D_MODEL)
    proj, small = _norm_matmul(x2d, w["norm_mix_g"], w["w_in"], w["w_small"], tm=1024, tn=1536)
    proj3 = proj.reshape(B, S, PROJ_COLS)
    o_att = _attention(proj3, *_rope_tables(S))
    o_dn = _deltanet(proj3, small.reshape(B, S, LANES), w["conv_qkv_w"], w["dn_par"], w["out_norm_g"])
    x1 = _mix(x2d, o_att.reshape(T, ATT_GROUP_WIDTH), o_dn.reshape(T, DN_WIDTH), proj,
              w["w_branch_a"], w["w_branch_b"], w["w_out"], tm=512)
    up = _norm_matmul(x1, w["norm_ffn_g"], w["w_up"], tm=1024, tn=D_FF)[0]
    y = _ffn_out(up, x1, w["ffn_conv_w"], w["ffn_conv_b"], w["w_down"], w["norm_final_g"], tm=512, S=S)
    return y.reshape(B, S, D_MODEL)


def kernel(x_prompt, x_sample, norm_mix_g, w_in, conv_qkv_w, a_log_f, a_log_b, dt_bias_f, dt_bias_b, out_norm_g, w_branch_a, w_branch_b, w_out, norm_ffn_g, w_up, ffn_conv_w, ffn_conv_b, w_down, norm_final_g):
    att = 3 * ATT_WIDTH
    dn_end = att + 3 * DN_WIDTH
    z_end = dn_end + DN_WIDTH
    small_end = z_end + 4 * DN_HEADS
    w_in_r = jnp.concatenate([
        w_in[:, small_end:],
        w_in[:, dn_end:z_end],
        w_in[:, att:dn_end],
        w_in[:, :ATT_WIDTH] * (ATT_HEAD_DIM ** -0.5),
        w_in[:, ATT_WIDTH:att],
    ], axis=1).astype(BF16)
    w_small = jnp.pad(w_in[:, z_end:small_end], ((0, 0), (0, LANES - 4 * DN_HEADS))).astype(BF16)
    par = jnp.zeros((8, LANES), F32)
    par = par.at[0, 2 * DN_HEADS:4 * DN_HEADS].set(jnp.concatenate([a_log_f, a_log_b]))
    par = par.at[1, 2 * DN_HEADS:4 * DN_HEADS].set(jnp.concatenate([dt_bias_f, dt_bias_b]))
    w = dict(
        norm_mix_g=norm_mix_g, w_in=w_in_r, w_small=w_small, conv_qkv_w=conv_qkv_w, dn_par=par, out_norm_g=out_norm_g,
        w_branch_a=w_branch_a.astype(BF16), w_branch_b=w_branch_b.astype(BF16), w_out=w_out.astype(BF16),
        norm_ffn_g=norm_ffn_g, w_up=w_up.astype(BF16), ffn_conv_w=ffn_conv_w, ffn_conv_b=ffn_conv_b,
        w_down=w_down.astype(BF16), norm_final_g=norm_final_g,
    )
    return _trunk(x_prompt, w), _trunk(x_sample, w)
```

```python
import functools

import jax
import jax.numpy as jnp
from jax import lax
from jax.experimental import pallas as pl
from jax.experimental.pallas import tpu as pltpu

F32 = jnp.float32
BF16 = jnp.bfloat16

D_MODEL = 1024
ATT_HEAD_DIM = 64
ATT_HEADS_PER_GROUP = 8
DILATIONS = (1, 4, 16)
ATT_HALF = 64
ATT_GROUP_WIDTH = ATT_HEADS_PER_GROUP * ATT_HEAD_DIM
ATT_WIDTH = len(DILATIONS) * ATT_GROUP_WIDTH
ROPE_DIM = ATT_HEAD_DIM // 4
ROPE_THETA = 500000.0
DN_HEADS = 8
DN_HEAD_DIM = 128
DN_WIDTH = DN_HEADS * DN_HEAD_DIM
SHORT_CONV = 5
D_FF = 2816
EPS = 1e-6
NEG_INF = -1e30
LOG2_E = 1.4426950408889634

LANES = 128
MIB = 1024 * 1024

COL_GATE_A = 0
COL_GATE_B = 1024
COL_Z = 2048
COL_DN_Q = 3072
COL_DN_K = 4096
COL_DN_V = 5120
COL_ATT_Q = 6144
COL_ATT_K = COL_ATT_Q + ATT_WIDTH
COL_ATT_V = COL_ATT_K + ATT_WIDTH
PROJ_COLS = COL_ATT_V + ATT_WIDTH


def _params(semantics, vmem_mib):
    return pltpu.CompilerParams(dimension_semantics=semantics, vmem_limit_bytes=vmem_mib * MIB)


def _mm(a, b):
    return jnp.dot(a.astype(BF16), b.astype(BF16), preferred_element_type=F32)


def _mm_nt(a, b):
    return lax.dot_general(a.astype(BF16), b.astype(BF16), (((1,), (1,)), ((), ())),
                           preferred_element_type=F32)


def _mm_tn(a, b):
    return lax.dot_general(a.astype(BF16), b.astype(BF16), (((0,), (0,)), ((), ())),
                           preferred_element_type=F32)


def _norm_matmul_kernel(*refs, row_chunk, with_side):
    if with_side:
        x_ref, g_ref, w_ref, ws_ref, o_ref, os_ref, h_ref = refs
    else:
        x_ref, g_ref, w_ref, o_ref, h_ref = refs

    @pl.when(pl.program_id(1) == 0)
    def _():
        def body(c, carry):
            r0 = pl.multiple_of(c * row_chunk, row_chunk)
            x = x_ref[pl.ds(r0, row_chunk), :]
            ms = jnp.mean(x * x, axis=-1, keepdims=True)
            h_ref[pl.ds(r0, row_chunk), :] = (x * lax.rsqrt(ms + EPS) * g_ref[...]).astype(BF16)
            return carry
        lax.fori_loop(0, x_ref.shape[0] // row_chunk, body, 0)
        if with_side:
            os_ref[...] = jnp.dot(h_ref[...], ws_ref[...], preferred_element_type=F32)

    o_ref[...] = jnp.dot(h_ref[...], w_ref[...], preferred_element_type=F32).astype(o_ref.dtype)


def _norm_matmul(x2d, gain, w_bf16, w_side=None, *, tm, tn):
    T, K = x2d.shape
    N = w_bf16.shape[1]
    with_side = w_side is not None
    in_specs = [
        pl.BlockSpec((tm, K), lambda i, j: (i, 0)),
        pl.BlockSpec((1, K), lambda i, j: (0, 0)),
        pl.BlockSpec((K, tn), lambda i, j: (0, j)),
    ]
    out_specs = [pl.BlockSpec((tm, tn), lambda i, j: (i, j))]
    out_shape = [jax.ShapeDtypeStruct((T, N), BF16)]
    args = [x2d, gain.reshape(1, K), w_bf16]
    if with_side:
        ns = w_side.shape[1]
        in_specs.append(pl.BlockSpec((K, ns), lambda i, j: (0, 0)))
        out_specs.append(pl.BlockSpec((tm, ns), lambda i, j: (i, 0)))
        out_shape.append(jax.ShapeDtypeStruct((T, ns), F32))
        args.append(w_side)
    return pl.pallas_call(
        functools.partial(_norm_matmul_kernel, row_chunk=128, with_side=with_side),
        grid=(T // tm, N // tn),
        in_specs=in_specs,
        out_specs=out_specs,
        out_shape=out_shape,
        scratch_shapes=[pltpu.VMEM((tm, K), BF16)],
        compiler_params=_params(("parallel", "arbitrary"), 48),
    )(*args)


ATT_QB = 128
ATT_UNROLL = 4


def _attn_group(q_ref, k_ref, v_ref, cos_ref, sin_ref, x32, bias_ref, qs, ks, vs, og_ref, lse_ref,
                *, S, dil, gi):
    L = S // dil
    rc = min(L, 256)
    nlc = L // rc

    def rows_of(start, n):
        return pl.ds(start, n, stride=dil) if dil > 1 else pl.ds(pl.multiple_of(start, ATT_QB), n)

    if dil > 1:
        def widen(c, carry):
            sl = pl.ds(pl.multiple_of(c * 256, 256), 256)
            for a, ref in enumerate((q_ref, k_ref, v_ref)):
                x32[a, sl, :] = ref[0, sl, :].astype(F32)
            return carry

        lax.fori_loop(0, S // 256, widen, 0)
        load = lambda a, rows: x32[a, rows, :]
    else:
        load = lambda a, rows: (q_ref, k_ref, v_ref)[a][0, rows, :].astype(F32)

    lane_in = lax.broadcasted_iota(jnp.int32, (LANES, LANES), 0)
    lane_out = lax.broadcasted_iota(jnp.int32, (LANES, LANES), 1)
    head_lane = lane_out % ATT_HEAD_DIM
    half = ROPE_DIM // 2
    rot_m = jnp.where((head_lane < half) & (lane_in == lane_out + half), -1.0,
                      jnp.where((head_lane >= half) & (head_lane < ROPE_DIM) & (lane_in == lane_out - half),
                                1.0, 0.0)).astype(BF16)

    def pre(idx, carry):
        r = idx // nlc if dil > 1 else 0
        lc = idx % nlc
        rows = rows_of(r + dil * lc * rc, rc)
        pos = r * L + lc * rc
        cs = cos_ref[rows, :]
        sn = sin_ref[rows, :]

        def rope(t):
            return t * cs + jnp.dot(t.astype(BF16), rot_m, preferred_element_type=F32) * sn

        q = rope(load(0, rows)) * LOG2_E
        ks[pl.ds(pl.multiple_of(pos, rc), rc), :] = rope(load(1, rows)).astype(BF16)
        vs[pl.ds(pl.multiple_of(pos, rc), rc), :] = load(2, rows).astype(BF16)
        head_a = lax.broadcasted_iota(jnp.int32, (rc, LANES), 1) < ATT_HEAD_DIM
        qa = jnp.where(head_a, q, 0.0).astype(BF16)
        qb = jnp.where(head_a, 0.0, q).astype(BF16)
        for sub in range(rc // ATT_QB):
            dst = pl.multiple_of(2 * pos + 2 * sub * ATT_QB, 2 * ATT_QB)
            qs[pl.ds(dst, ATT_QB), :] = qa[sub * ATT_QB:(sub + 1) * ATT_QB]
            qs[pl.ds(dst + ATT_QB, ATT_QB), :] = qb[sub * ATT_QB:(sub + 1) * ATT_QB]
        return carry

    lax.fori_loop(0, dil * nlc, pre, 0, unroll=2)

    nb = L // ATT_QB
    nk = min(2 * ATT_QB, L)
    col = lax.broadcasted_iota(jnp.int32, (ATT_QB, nk), 1)
    row = lax.broadcasted_iota(jnp.int32, (ATT_QB, nk), 0)
    head_a = lax.broadcasted_iota(jnp.int32, (ATT_QB, LANES), 1) < ATT_HEAD_DIM
    for v in range(3):
        band = jnp.where(jnp.abs(col - row - v * ATT_HALF) <= ATT_HALF, 0.0, NEG_INF)
        bias_ref[v, 0:ATT_QB, 0:nk] = band
        bias_ref[v, ATT_QB:2 * ATT_QB, 0:nk] = band

    def both(t):
        return jnp.where(head_a, t[:ATT_QB], t[ATT_QB:])

    def blk(it, carry):
        work = []
        for ub in range(ATT_UNROLL):
            idx = it * ATT_UNROLL + ub
            r = idx // nb if dil > 1 else 0
            n = idx % nb
            q0 = n * ATT_QB
            start = jnp.clip(q0 - ATT_HALF, 0, L - nk)
            qsl = pl.ds(pl.multiple_of(2 * (r * L + q0), 2 * ATT_QB), 2 * ATT_QB)
            ksl = pl.ds(pl.multiple_of(r * L + start, ATT_HALF), nk)
            s = lax.dot_general(qs[qsl, :], ks[ksl, :], (((1,), (1,)), ((), ())), preferred_element_type=F32)
            work.append(dict(rows=rows_of(r + dil * q0, ATT_QB), ksl=ksl, s=s,
                             variant=(q0 - start) // ATT_HALF))
        for w in work:
            s = w["s"] + bias_ref[w["variant"], :, 0:nk]
            w["m"] = jnp.max(s, axis=1, keepdims=True)
            p = jnp.exp2(s - w["m"])
            w["l"] = jnp.sum(p, axis=1, keepdims=True)
            w["p"] = p.astype(BF16)
        for w in work:
            w["pv"] = jnp.dot(w["p"], vs[w["ksl"], :], preferred_element_type=F32)
        for w in work:
            l = both(w["l"])
            og_ref[gi, w["rows"], :] = both(w["pv"]) / l
            lse_ref[gi, w["rows"], :] = both(w["m"]) + jnp.log2(l)
        return carry

    lax.fori_loop(0, dil * nb // ATT_UNROLL, blk, 0)


def _attn_kernel(q_ref, k_ref, v_ref, cos_ref, sin_ref, o_ref,
                 x32, bias_ref, qs, ks, vs, og_ref, lse_ref, *, S):
    g = pl.program_id(2)
    for gi, dil in enumerate(DILATIONS):
        @pl.when(g == gi)
        def _(gi=gi, dil=dil):
            _attn_group(q_ref, k_ref, v_ref, cos_ref, sin_ref, x32, bias_ref, qs, ks, vs, og_ref, lse_ref,
                        S=S, dil=dil, gi=gi)

    @pl.when(g == len(DILATIONS) - 1)
    def _():
        rc = 256

        def fin(c, carry):
            sl = pl.ds(pl.multiple_of(c * rc, rc), rc)
            lse = [lse_ref[gi, sl, :] for gi in range(len(DILATIONS))]
            top = functools.reduce(jnp.maximum, lse)
            wts = [jnp.exp2(t - top) for t in lse]
            num = sum(wt * og_ref[gi, sl, :] for gi, wt in enumerate(wts))
            o_ref[0, sl, :] = (num / sum(wts)).astype(o_ref.dtype)
            return carry

        lax.fori_loop(0, S // rc, fin, 0)


def _attention(proj3, cos_t, sin_t):
    B, S, _ = proj3.shape
    ng = len(DILATIONS)

    def col_spec(col0):
        base = col0 // LANES
        per_group = ATT_GROUP_WIDTH // LANES
        return pl.BlockSpec((1, S, LANES), lambda b, j, g: (b, 0, base + g * per_group + j))

    tab_spec = pl.BlockSpec((S, LANES), lambda b, j, g: (0, 0))
    seq_bf16 = pltpu.VMEM((S, LANES), BF16)
    return pl.pallas_call(
        functools.partial(_attn_kernel, S=S),
        grid=(B, ATT_GROUP_WIDTH // LANES, ng),
        in_specs=[col_spec(COL_ATT_Q), col_spec(COL_ATT_K), col_spec(COL_ATT_V), tab_spec, tab_spec],
        out_specs=pl.BlockSpec((1, S, LANES), lambda b, j, g: (b, 0, j)),
        out_shape=jax.ShapeDtypeStruct((B, S, ATT_GROUP_WIDTH), BF16),
        scratch_shapes=[pltpu.VMEM((3, S, LANES), F32),
                        pltpu.VMEM((3, 2 * ATT_QB, 2 * ATT_QB), F32),
                        pltpu.VMEM((2 * S, LANES), BF16), seq_bf16, seq_bf16,
                        pltpu.VMEM((ng, S, LANES), F32), pltpu.VMEM((ng, S, LANES), F32)],
        compiler_params=_params(("parallel", "parallel", "arbitrary"), 48),
    )(proj3, proj3, proj3, cos_t, sin_t)


def _rope_tables(S):
    half = ROPE_DIM // 2
    inv = ROPE_THETA ** (-jnp.arange(half, dtype=F32) / half)
    ang = jnp.arange(S, dtype=F32)[:, None] * inv[None, :]
    cos, sin = jnp.cos(ang), jnp.sin(ang)
    pad = ATT_HEAD_DIM - ROPE_DIM
    cos_h = jnp.concatenate([cos, cos, jnp.ones((S, pad), F32)], axis=1)
    sin_h = jnp.concatenate([sin, sin, jnp.zeros((S, pad), F32)], axis=1)
    tile = lambda t: jnp.concatenate([t, t], axis=1)
    return tile(cos_h), tile(sin_h)


DN_BLK = 128
DN_UNROLL = 8
DN_DOUBLINGS = 5


def _dn_kernel(q_ref, k_ref, v_ref, z_ref, sm_ref, cwq_ref, cwk_ref, cwv_ref, par_ref, ng_ref, o_ref,
               xp, qn, kn, vn, kq_s, b_s, dec_s, oacc, *, S):
    h = pl.program_id(1)
    rc = 256
    halo = 8

    xp[0:halo, :] = jnp.zeros((halo, LANES), F32)
    xp[S + halo:S + 2 * halo, :] = jnp.zeros((halo, LANES), F32)
    for src, cw, dst, mode in ((q_ref, cwq_ref, qn, "q"), (k_ref, cwk_ref, kn, "k"), (v_ref, cwv_ref, vn, "v")):
        def cp(c, carry, src=src):
            r0 = pl.multiple_of(c * rc, rc)
            xp[pl.ds(r0 + halo, rc), :] = src[0, pl.ds(r0, rc), :].astype(F32)
            return carry

        lax.fori_loop(0, S // rc, cp, 0)

        def cv(c, carry, cw=cw, dst=dst, mode=mode):
            r0 = pl.multiple_of(c * rc, rc)
            off = halo - SHORT_CONV // 2
            y = xp[pl.ds(r0 + off, rc), :] * cw[0:1, :]
            for i in range(1, SHORT_CONV):
                y = y + xp[pl.ds(r0 + off + i, rc), :] * cw[i:i + 1, :]
            y = y * jax.nn.sigmoid(y)
            if mode != "v":
                y = y * lax.rsqrt(jnp.sum(y * y, axis=-1, keepdims=True) + EPS)
            if mode == "q":
                y = y * (DN_HEAD_DIM ** -0.5)
            dst[pl.ds(r0, rc), :] = y
            return carry

        lax.fori_loop(0, S // rc, cv, 0, unroll=2)

    lane = lax.broadcasted_iota(jnp.int32, (DN_BLK, DN_BLK), 1)
    rowi = lax.broadcasted_iota(jnp.int32, (DN_BLK, DN_BLK), 0)
    eye = jnp.where(lane == rowi, 1.0, 0.0)
    same_half = (rowi >= DN_BLK // 2) == (lane >= DN_BLK // 2)
    a_exp = jnp.exp(par_ref[0:1, :])
    dt_bias = par_ref[1:2, :]
    n_blocks = S // DN_BLK

    def masks(d):
        return (lane <= rowi, lane < rowi) if d == 0 else (lane >= rowi, lane > rowi)

    tri3 = [jnp.concatenate([jnp.where(masks(d)[0], 1.0, 0.0).astype(BF16)] * 3, axis=1) for d in range(2)]

    def prepass(c, carry):
        chains = []
        for uu in range(DN_UNROLL):
            blk = c * DN_UNROLL + uu
            rows = pl.ds(pl.multiple_of(blk * DN_BLK, DN_BLK), DN_BLK)
            sm = sm_ref[0, rows, :]
            beta_all = jax.nn.sigmoid(sm)
            g_all = -a_exp * jax.nn.softplus(sm + dt_bias)
            q = qn[rows, :]
            k = kn[rows, :]
            v = vn[rows, :]
            for d in range(2):
                beta = jnp.sum(jnp.where(lane == h + DN_HEADS * d, beta_all, 0.0), axis=1, keepdims=True)
                g = jnp.sum(jnp.where(lane == h + DN_HEADS * (2 + d), g_all, 0.0), axis=1, keepdims=True)
                gb = jnp.broadcast_to(g, (DN_BLK, DN_BLK))
                hi = gb.astype(BF16)
                r1 = gb - hi.astype(F32)
                mid = r1.astype(BF16)
                lo = (r1 - mid.astype(F32)).astype(BF16)
                chains.append(dict(blk=blk, rows=rows, d=d, q=q, k=k, kb=k * beta, vb=v * beta,
                                   g3=jnp.concatenate([hi, mid, lo], axis=0)))
            fwd, bwd = chains[-2], chains[-1]
            kk = _mm_nt(jnp.concatenate([fwd["kb"], bwd["kb"], q], axis=0), k)
            fwd["kk"], bwd["kk"] = kk[:DN_BLK], kk[DN_BLK:2 * DN_BLK]
            fwd["qk"] = bwd["qk"] = kk[2 * DN_BLK:]
        for ch in chains:
            d = ch["d"]
            tri = masks(d)[0]
            gc = jnp.dot(tri3[d], ch["g3"], preferred_element_type=F32)
            last = DN_BLK - 1 if d == 0 else 0
            ch["gc"], ch["gl"] = gc, gc[last:last + 1, :]
            ch["dm"] = jnp.where(tri, jnp.exp(jnp.where(tri, gc - gc.T, 0.0)), 0.0)
        for ch in chains:
            nm = jnp.where(masks(ch["d"])[1], -(ch["kk"] * ch["dm"]), 0.0)
            ch["nd"] = jnp.where(same_half, nm, 0.0)
            ch["no"] = jnp.where(same_half, 0.0, nm)
        for ch in chains:
            ch["x"] = eye + ch["nd"]
            ch["p"] = _mm(ch["nd"], ch["nd"])
        for _ in range(DN_DOUBLINGS - 1):
            for ch in chains:
                r = _mm(jnp.concatenate([ch["x"], ch["p"]], axis=0), ch["p"])
                ch["x"] = ch["x"] + r[:DN_BLK]
                ch["p"] = r[DN_BLK:]
        for ch in chains:
            ch["x"] = ch["x"] + _mm(ch["x"], ch["p"])
            ch["y"] = _mm(ch["no"], ch["x"])
        for ch in chains:
            eg = jnp.exp(ch["gc"])
            x = ch["x"] + _mm(ch["x"], ch["y"])
            ch["wu"] = _mm(x, jnp.concatenate([ch["kb"] * eg, ch["vb"]], axis=1))
            ch["qg"] = ch["q"] * eg
            ch["kg"] = ch["k"] * jnp.exp(ch["gl"] - ch["gc"])
        for ch in chains:
            iw = _mm(ch["qk"] * ch["dm"], ch["wu"])
            ch["qp"] = ch["qg"] - iw[:, :LANES]
            ch["oc"] = iw[:, LANES:]
        for ch in chains:
            d, blk = ch["d"], ch["blk"]
            kb_ = _mm_tn(ch["kg"], ch["wu"])
            base = pl.multiple_of(blk * 2 * DN_BLK, 2 * DN_BLK)
            kq_s[d, pl.ds(base, DN_BLK), :] = (-kb_[:, :LANES]).astype(BF16)
            kq_s[d, pl.ds(base + DN_BLK, DN_BLK), :] = ch["qp"].astype(BF16)
            b_s[d, ch["rows"], :] = kb_[:, LANES:]
            dec_s[d, pl.ds(pl.multiple_of(blk * 8, 8), 8), :] = jnp.broadcast_to(jnp.exp(ch["gl"]), (8, LANES))
        for uu in range(DN_UNROLL):
            oacc[chains[2 * uu]["rows"], :] = chains[2 * uu]["oc"] + chains[2 * uu + 1]["oc"]
        return carry

    lax.fori_loop(0, n_blocks // DN_UNROLL, prepass, 0)

    def step(d, blk, state):
        base = pl.multiple_of(blk * 2 * DN_BLK, 2 * DN_BLK)
        r = jnp.dot(kq_s[d, pl.ds(base, 2 * DN_BLK), :], state.astype(BF16), preferred_element_type=F32)
        rows = pl.ds(pl.multiple_of(blk * DN_BLK, DN_BLK), DN_BLK)
        oacc[rows, :] += r[DN_BLK:]
        dec = dec_s[d, pl.ds(pl.multiple_of(blk * 8, 8), 1), :]
        return state * dec + r[:DN_BLK] + b_s[d, rows, :]

    def rec(t, carry):
        sf, sb = carry
        bf = 2 * t
        bb = n_blocks - 1 - 2 * t
        sf = step(0, bf, sf)
        sb = step(1, bb, sb)
        sf = step(0, bf + 1, sf)
        sb = step(1, bb - 1, sb)
        return sf, sb

    zero_state = jnp.zeros((DN_HEAD_DIM, DN_HEAD_DIM), F32)
    lax.fori_loop(0, n_blocks // 2, rec, (zero_state, zero_state))

    def fin(c, carry):
        sl = pl.ds(pl.multiple_of(c * rc, rc), rc)
        o = oacc[sl, :]
        o = o * lax.rsqrt(jnp.mean(o * o, axis=-1, keepdims=True) + EPS) * ng_ref[...]
        z = z_ref[0, sl, :].astype(F32)
        o_ref[0, sl, :] = (o * (z * jax.nn.sigmoid(z))).astype(o_ref.dtype)
        return carry

    lax.fori_loop(0, S // rc, fin, 0)


def _deltanet(proj3, small3, conv_w, par, norm_g):
    B, S, _ = proj3.shape

    def col_spec(col0):
        base = col0 // LANES
        return pl.BlockSpec((1, S, LANES), lambda b, h: (b, 0, base + h))

    def conv_spec(part):
        return pl.BlockSpec((SHORT_CONV, LANES), lambda b, h: (0, part * DN_HEADS + h))

    seq_f32 = pltpu.VMEM((S, LANES), F32)
    return pl.pallas_call(
        functools.partial(_dn_kernel, S=S),
        grid=(B, DN_HEADS),
        in_specs=[col_spec(COL_DN_Q), col_spec(COL_DN_K), col_spec(COL_DN_V), col_spec(COL_Z),
                  pl.BlockSpec((1, S, LANES), lambda b, h: (b, 0, 0)),
                  conv_spec(0), conv_spec(1), conv_spec(2),
                  pl.BlockSpec((8, LANES), lambda b, h: (0, 0)),
                  pl.BlockSpec((1, LANES), lambda b, h: (0, 0))],
        out_specs=pl.BlockSpec((1, S, LANES), lambda b, h: (b, 0, h)),
        out_shape=jax.ShapeDtypeStruct((B, S, DN_WIDTH), BF16),
        scratch_shapes=[pltpu.VMEM((S + 16, LANES), F32), seq_f32, seq_f32, seq_f32,
                        pltpu.VMEM((2, 2 * S, LANES), BF16),
                        pltpu.VMEM((2, S, LANES), F32),
                        pltpu.VMEM((2, S // DN_BLK * 8, LANES), F32),
                        seq_f32],
        compiler_params=_params(("parallel", "parallel"), 56),
    )(proj3, proj3, proj3, proj3, small3, conv_w, conv_w, conv_w, par, norm_g.reshape(1, LANES))


def _mix_kernel(x_ref, oa_ref, ob_ref, ga_ref, gb_ref, wa_ref, wb_ref, wo_ref, o_ref):
    ya = jnp.dot(oa_ref[...], wa_ref[...], preferred_element_type=F32)
    yb = jnp.dot(ob_ref[...], wb_ref[...], preferred_element_type=F32)
    mix = jax.nn.sigmoid(ga_ref[...].astype(F32)) * ya + jax.nn.sigmoid(gb_ref[...].astype(F32)) * yb
    o_ref[...] = x_ref[...] + jnp.dot(mix.astype(BF16), wo_ref[...], preferred_element_type=F32)


def _mix(x2d, o_att, o_dn, proj, wa, wb, wo, *, tm):
    T = x2d.shape[0]
    row = lambda width, cb=0: pl.BlockSpec((tm, width), lambda i: (i, cb))
    full = lambda a: pl.BlockSpec(a.shape, lambda i: (0, 0))
    return pl.pallas_call(
        _mix_kernel,
        grid=(T // tm,),
        in_specs=[row(D_MODEL), row(ATT_GROUP_WIDTH), row(DN_WIDTH),
                  row(D_MODEL, COL_GATE_A // D_MODEL), row(D_MODEL, COL_GATE_B // D_MODEL),
                  full(wa), full(wb), full(wo)],
        out_specs=row(D_MODEL),
        out_shape=jax.ShapeDtypeStruct((T, D_MODEL), F32),
        compiler_params=_params(("parallel",), 48),
    )(x2d, o_att, o_dn, proj, proj, wa, wb, wo)


FFN_KC = 256
FFN_HALO = 16


def _ffn_out_kernel(g_ref, v_ref, gp_ref, gn_ref, x_ref, cw_ref, cb_ref, wd_ref, nf_ref, o_ref, act_ref,
                    *, tiles_per_seq):
    i = pl.program_id(0)
    tm = g_ref.shape[0]
    pos = i % tiles_per_seq
    keep_prev = jnp.where(pos == 0, 0.0, 1.0)
    keep_next = jnp.where(pos == tiles_per_seq - 1, 0.0, 1.0)
    row = lax.broadcasted_iota(jnp.int32, (tm, FFN_KC), 0)
    for c in range(D_FF // FFN_KC):
        sl = slice(c * FFN_KC, (c + 1) * FFN_KC)
        g = g_ref[:, sl].astype(F32)
        prev_row = gp_ref[FFN_HALO - 1:FFN_HALO, sl].astype(F32) * keep_prev
        next_row = gn_ref[0:1, sl].astype(F32) * keep_next
        g_prev = jnp.where(row == 0, prev_row, pltpu.roll(g, 1, 0))
        g_next = jnp.where(row == tm - 1, next_row, pltpu.roll(g, tm - 1, 0))
        conv = g_prev * cw_ref[0:1, sl] + g * cw_ref[1:2, sl] + g_next * cw_ref[2:3, sl] + cb_ref[:, sl]
        gelu = 0.5 * conv * (1.0 + lax.erf(conv * (2.0 ** -0.5)))
        act_ref[:, sl] = (gelu * v_ref[:, sl].astype(F32)).astype(BF16)
    x2 = x_ref[...] + jnp.dot(act_ref[...], wd_ref[...], preferred_element_type=F32)
    ms = jnp.mean(x2 * x2, axis=-1, keepdims=True)
    o_ref[...] = x2 * lax.rsqrt(ms + EPS) * nf_ref[...]


def _ffn_out(up, x1, conv_w, conv_b, wd, norm_g, *, tm, S):
    T = x1.shape[0]
    tiles_per_seq = S // tm
    hb = tm // FFN_HALO
    last_hb = T // FFN_HALO - 1
    return pl.pallas_call(
        functools.partial(_ffn_out_kernel, tiles_per_seq=tiles_per_seq),
        grid=(T // tm,),
        in_specs=[
            pl.BlockSpec((tm, D_FF), lambda i: (i, 0)),
            pl.BlockSpec((tm, D_FF), lambda i: (i, 1)),
            pl.BlockSpec((FFN_HALO, D_FF), lambda i: (jnp.maximum(i * hb - 1, 0), 0)),
            pl.BlockSpec((FFN_HALO, D_FF), lambda i: (jnp.minimum((i + 1) * hb, last_hb), 0)),
            pl.BlockSpec((tm, D_MODEL), lambda i: (i, 0)),
            pl.BlockSpec((3, D_FF), lambda i: (0, 0)),
            pl.BlockSpec((1, D_FF), lambda i: (0, 0)),
            pl.BlockSpec((D_FF, D_MODEL), lambda i: (0, 0)),
            pl.BlockSpec((1, D_MODEL), lambda i: (0, 0)),
        ],
        out_specs=pl.BlockSpec((tm, D_MODEL), lambda i: (i, 0)),
        out_shape=jax.ShapeDtypeStruct((T, D_MODEL), F32),
        scratch_shapes=[pltpu.VMEM((tm, D_FF), BF16)],
        compiler_params=_params(("parallel",), 56),
    )(up, up, up, up, x1, conv_w, conv_b.reshape(1, D_FF), wd, norm_g.reshape(1, D_MODEL))


def _trunk(x, w):
    B, S, _ = x.shape
    T = B * S
    x2d = x.reshape(T, D_MODEL)
    proj, small = _norm_matmul(x2d, w["norm_mix_g"], w["w_in"], w["w_small"], tm=1024, tn=1536)
    proj3 = proj.reshape(B, S, PROJ_COLS)
    o_att = _attention(proj3, *_rope_tables(S))
    o_dn = _deltanet(proj3, small.reshape(B, S, LANES), w["conv_qkv_w"], w["dn_par"], w["out_norm_g"])
    x1 = _mix(x2d, o_att.reshape(T, ATT_GROUP_WIDTH), o_dn.reshape(T, DN_WIDTH), proj,
              w["w_branch_a"], w["w_branch_b"], w["w_out"], tm=512)
    up = _norm_matmul(x1, w["norm_ffn_g"], w["w_up"], tm=1024, tn=D_FF)[0]
    y = _ffn_out(up, x1, w["ffn_conv_w"], w["ffn_conv_b"], w["w_down"], w["norm_final_g"], tm=512, S=S)
    return y.reshape(B, S, D_MODEL)


def kernel(x_prompt, x_sample, norm_mix_g, w_in, conv_qkv_w, a_log_f, a_log_b, dt_bias_f, dt_bias_b, out_norm_g, w_branch_a, w_branch_b, w_out, norm_ffn_g, w_up, ffn_conv_w, ffn_conv_b, w_down, norm_final_g):
    att = 3 * ATT_WIDTH
    dn_end = att + 3 * DN_WIDTH
    z_end = dn_end + DN_WIDTH
    small_end = z_end + 4 * DN_HEADS
    w_in_r = jnp.concatenate([
        w_in[:, small_end:],
        w_in[:, dn_end:z_end],
        w_in[:, att:dn_end],
        w_in[:, :ATT_WIDTH] * (ATT_HEAD_DIM ** -0.5),
        w_in[:, ATT_WIDTH:att],
    ], axis=1).astype(BF16)
    w_small = jnp.pad(w_in[:, z_end:small_end], ((0, 0), (0, LANES - 4 * DN_HEADS))).astype(BF16)
    par = jnp.zeros((8, LANES), F32)
    par = par.at[0, 2 * DN_HEADS:4 * DN_HEADS].set(jnp.concatenate([a_log_f, a_log_b]))
    par = par.at[1, 2 * DN_HEADS:4 * DN_HEADS].set(jnp.concatenate([dt_bias_f, dt_bias_b]))
    w = dict(
        norm_mix_g=norm_mix_g, w_in=w_in_r, w_small=w_small, conv_qkv_w=conv_qkv_w, dn_par=par, out_norm_g=out_norm_g,
        w_branch_a=w_branch_a.astype(BF16), w_branch_b=w_branch_b.astype(BF16), w_out=w_out.astype(BF16),
        norm_ffn_g=norm_ffn_g, w_up=w_up.astype(BF16), ffn_conv_w=ffn_conv_w, ffn_conv_b=ffn_conv_b,
        w_down=w_down.astype(BF16), norm_final_g=norm_final_g,
    )
    return _trunk(x_prompt, w), _trunk(x_sample, w)
```

```python
import functools

import jax
import jax.numpy as jnp
from jax import lax
from jax.experimental import pallas as pl
from jax.experimental.pallas import tpu as pltpu

F32 = jnp.float32
BF16 = jnp.bfloat16

D_MODEL = 1024
ATT_HEAD_DIM = 64
ATT_HEADS_PER_GROUP = 8
DILATIONS = (1, 4, 16)
ATT_HALF = 64
ATT_GROUP_WIDTH = ATT_HEADS_PER_GROUP * ATT_HEAD_DIM
ATT_WIDTH = len(DILATIONS) * ATT_GROUP_WIDTH
ROPE_DIM = ATT_HEAD_DIM // 4
ROPE_THETA = 500000.0
DN_HEADS = 8
DN_HEAD_DIM = 128
DN_WIDTH = DN_HEADS * DN_HEAD_DIM
SHORT_CONV = 5
D_FF = 2816
EPS = 1e-6
NEG_INF = -1e30
LOG2_E = 1.4426950408889634

LANES = 128
MIB = 1024 * 1024

COL_GATE_A = 0
COL_GATE_B = 1024
COL_Z = 2048
COL_DN_Q = 3072
COL_DN_K = 4096
COL_DN_V = 5120
COL_ATT_Q = 6144
COL_ATT_K = COL_ATT_Q + ATT_WIDTH
COL_ATT_V = COL_ATT_K + ATT_WIDTH
PROJ_COLS = COL_ATT_V + ATT_WIDTH


def _params(semantics, vmem_mib):
    return pltpu.CompilerParams(dimension_semantics=semantics, vmem_limit_bytes=vmem_mib * MIB)


def _mm(a, b):
    return jnp.dot(a.astype(BF16), b.astype(BF16), preferred_element_type=F32)


def _mm_nt(a, b):
    return lax.dot_general(a.astype(BF16), b.astype(BF16), (((1,), (1,)), ((), ())),
                           preferred_element_type=F32)


def _mm_tn(a, b):
    return lax.dot_general(a.astype(BF16), b.astype(BF16), (((0,), (0,)), ((), ())),
                           preferred_element_type=F32)


def _norm_matmul_kernel(*refs, row_chunk, with_side):
    if with_side:
        x_ref, g_ref, w_ref, ws_ref, o_ref, os_ref, h_ref = refs
    else:
        x_ref, g_ref, w_ref, o_ref, h_ref = refs

    @pl.when(pl.program_id(1) == 0)
    def _():
        def body(c, carry):
            r0 = pl.multiple_of(c * row_chunk, row_chunk)
            x = x_ref[pl.ds(r0, row_chunk), :]
            ms = jnp.mean(x * x, axis=-1, keepdims=True)
            h_ref[pl.ds(r0, row_chunk), :] = (x * lax.rsqrt(ms + EPS) * g_ref[...]).astype(BF16)
            return carry
        lax.fori_loop(0, x_ref.shape[0] // row_chunk, body, 0, unroll=2)
        if with_side:
            os_ref[...] = jnp.dot(h_ref[...], ws_ref[...], preferred_element_type=F32)

    o_ref[...] = jnp.dot(h_ref[...], w_ref[...], preferred_element_type=F32).astype(o_ref.dtype)


def _norm_matmul(x2d, gain, w_bf16, w_side=None, *, tm, tn):
    T, K = x2d.shape
    N = w_bf16.shape[1]
    with_side = w_side is not None
    in_specs = [
        pl.BlockSpec((tm, K), lambda i, j: (i, 0)),
        pl.BlockSpec((1, K), lambda i, j: (0, 0)),
        pl.BlockSpec((K, tn), lambda i, j: (0, j)),
    ]
    out_specs = [pl.BlockSpec((tm, tn), lambda i, j: (i, j))]
    out_shape = [jax.ShapeDtypeStruct((T, N), BF16)]
    args = [x2d, gain.reshape(1, K), w_bf16]
    if with_side:
        ns = w_side.shape[1]
        in_specs.append(pl.BlockSpec((K, ns), lambda i, j: (0, 0)))
        out_specs.append(pl.BlockSpec((tm, ns), lambda i, j: (i, 0)))
        out_shape.append(jax.ShapeDtypeStruct((T, ns), F32))
        args.append(w_side)
    return pl.pallas_call(
        functools.partial(_norm_matmul_kernel, row_chunk=128, with_side=with_side),
        grid=(T // tm, N // tn),
        in_specs=in_specs,
        out_specs=out_specs,
        out_shape=out_shape,
        scratch_shapes=[pltpu.VMEM((tm, K), BF16)],
        compiler_params=_params(("parallel", "arbitrary"), 48),
    )(*args)


ATT_QB = 128
ATT_UNROLL = 4


def _attn_group(q_ref, k_ref, v_ref, cos_ref, sin_ref, x32, bias_ref, qs, ks, vs, og_ref, lse_ref,
                *, S, dil, gi):
    L = S // dil
    rc = min(L, 256)
    nlc = L // rc

    def rows_of(start, n):
        return pl.ds(start, n, stride=dil) if dil > 1 else pl.ds(pl.multiple_of(start, ATT_QB), n)

    if dil > 1:
        def widen(c, carry):
            sl = pl.ds(pl.multiple_of(c * 256, 256), 256)
            for a, ref in enumerate((q_ref, k_ref, v_ref)):
                x32[a, sl, :] = ref[0, sl, :].astype(F32)
            return carry

        lax.fori_loop(0, S // 256, widen, 0)
        load = lambda a, rows: x32[a, rows, :]
    else:
        load = lambda a, rows: (q_ref, k_ref, v_ref)[a][0, rows, :].astype(F32)

    lane_in = lax.broadcasted_iota(jnp.int32, (LANES, LANES), 0)
    lane_out = lax.broadcasted_iota(jnp.int32, (LANES, LANES), 1)
    head_lane = lane_out % ATT_HEAD_DIM
    half = ROPE_DIM // 2
    rot_m = jnp.where((head_lane < half) & (lane_in == lane_out + half), -1.0,
                      jnp.where((head_lane >= half) & (head_lane < ROPE_DIM) & (lane_in == lane_out - half),
                                1.0, 0.0)).astype(BF16)

    def pre(idx, carry):
        r = idx // nlc if dil > 1 else 0
        lc = idx % nlc
        rows = rows_of(r + dil * lc * rc, rc)
        pos = r * L + lc * rc
        cs = cos_ref[rows, :]
        sn = sin_ref[rows, :]

        def rope(t):
            return t * cs + jnp.dot(t.astype(BF16), rot_m, preferred_element_type=F32) * sn

        q = rope(load(0, rows)) * LOG2_E
        ks[pl.ds(pl.multiple_of(pos, rc), rc), :] = rope(load(1, rows)).astype(BF16)
        vs[pl.ds(pl.multiple_of(pos, rc), rc), :] = load(2, rows).astype(BF16)
        head_a = lax.broadcasted_iota(jnp.int32, (rc, LANES), 1) < ATT_HEAD_DIM
        qa = jnp.where(head_a, q, 0.0).astype(BF16)
        qb = jnp.where(head_a, 0.0, q).astype(BF16)
        for sub in range(rc // ATT_QB):
            dst = pl.multiple_of(2 * pos + 2 * sub * ATT_QB, 2 * ATT_QB)
            qs[pl.ds(dst, ATT_QB), :] = qa[sub * ATT_QB:(sub + 1) * ATT_QB]
            qs[pl.ds(dst + ATT_QB, ATT_QB), :] = qb[sub * ATT_QB:(sub + 1) * ATT_QB]
        return carry

    lax.fori_loop(0, dil * nlc, pre, 0, unroll=2 * 256 // rc)

    nb = L // ATT_QB
    nk = min(2 * ATT_QB, L)
    col = lax.broadcasted_iota(jnp.int32, (ATT_QB, nk), 1)
    row = lax.broadcasted_iota(jnp.int32, (ATT_QB, nk), 0)
    head_a = lax.broadcasted_iota(jnp.int32, (ATT_QB, LANES), 1) < ATT_HEAD_DIM
    for v in range(3):
        band = jnp.where(jnp.abs(col - row - v * ATT_HALF) <= ATT_HALF, 0.0, NEG_INF)
        bias_ref[v, 0:ATT_QB, 0:nk] = band
        bias_ref[v, ATT_QB:2 * ATT_QB, 0:nk] = band

    def both(t):
        return jnp.where(head_a, t[:ATT_QB], t[ATT_QB:])

    def blk(it, carry):
        work = []
        for ub in range(ATT_UNROLL):
            idx = it * ATT_UNROLL + ub
            r = idx // nb if dil > 1 else 0
            n = idx % nb
            q0 = n * ATT_QB
            start = jnp.clip(q0 - ATT_HALF, 0, L - nk)
            qsl = pl.ds(pl.multiple_of(2 * (r * L + q0), 2 * ATT_QB), 2 * ATT_QB)
            ksl = pl.ds(pl.multiple_of(r * L + start, ATT_HALF), nk)
            s = lax.dot_general(qs[qsl, :], ks[ksl, :], (((1,), (1,)), ((), ())), preferred_element_type=F32)
            work.append(dict(rows=rows_of(r + dil * q0, ATT_QB), ksl=ksl, s=s,
                             variant=(q0 - start) // ATT_HALF))
        for w in work:
            s = w["s"] + bias_ref[w["variant"], :, 0:nk]
            w["m"] = jnp.max(s, axis=1, keepdims=True)
            p = jnp.exp2(s - w["m"])
            w["l"] = jnp.sum(p, axis=1, keepdims=True)
            w["p"] = p.astype(BF16)
        for w in work:
            w["pv"] = jnp.dot(w["p"], vs[w["ksl"], :], preferred_element_type=F32)
        for w in work:
            l = both(w["l"])
            og_ref[gi, w["rows"], :] = both(w["pv"]) / l
            lse_ref[gi, w["rows"], :] = both(w["m"]) + jnp.log2(l)
        return carry

    lax.fori_loop(0, dil * nb // ATT_UNROLL, blk, 0)


def _attn_kernel(q_ref, k_ref, v_ref, cos_ref, sin_ref, o_ref,
                 x32, bias_ref, qs, ks, vs, og_ref, lse_ref, *, S):
    g = pl.program_id(2)
    for gi, dil in enumerate(DILATIONS):
        @pl.when(g == gi)
        def _(gi=gi, dil=dil):
            _attn_group(q_ref, k_ref, v_ref, cos_ref, sin_ref, x32, bias_ref, qs, ks, vs, og_ref, lse_ref,
                        S=S, dil=dil, gi=gi)

    @pl.when(g == len(DILATIONS) - 1)
    def _():
        rc = 256

        def fin(c, carry):
            sl = pl.ds(pl.multiple_of(c * rc, rc), rc)
            lse = [lse_ref[gi, sl, :] for gi in range(len(DILATIONS))]
            top = functools.reduce(jnp.maximum, lse)
            wts = [jnp.exp2(t - top) for t in lse]
            num = sum(wt * og_ref[gi, sl, :] for gi, wt in enumerate(wts))
            o_ref[0, sl, :] = (num / sum(wts)).astype(o_ref.dtype)
            return carry

        lax.fori_loop(0, S // rc, fin, 0, unroll=2)


def _attention(proj3, cos_t, sin_t):
    B, S, _ = proj3.shape
    ng = len(DILATIONS)

    def col_spec(col0):
        base = col0 // LANES
        per_group = ATT_GROUP_WIDTH // LANES
        return pl.BlockSpec((1, S, LANES), lambda b, j, g: (b, 0, base + g * per_group + j))

    tab_spec = pl.BlockSpec((S, LANES), lambda b, j, g: (0, 0))
    seq_bf16 = pltpu.VMEM((S, LANES), BF16)
    return pl.pallas_call(
        functools.partial(_attn_kernel, S=S),
        grid=(B, ATT_GROUP_WIDTH // LANES, ng),
        in_specs=[col_spec(COL_ATT_Q), col_spec(COL_ATT_K), col_spec(COL_ATT_V), tab_spec, tab_spec],
        out_specs=pl.BlockSpec((1, S, LANES), lambda b, j, g: (b, 0, j)),
        out_shape=jax.ShapeDtypeStruct((B, S, ATT_GROUP_WIDTH), BF16),
        scratch_shapes=[pltpu.VMEM((3, S, LANES), F32),
                        pltpu.VMEM((3, 2 * ATT_QB, 2 * ATT_QB), F32),
                        pltpu.VMEM((2 * S, LANES), BF16), seq_bf16, seq_bf16,
                        pltpu.VMEM((ng, S, LANES), F32), pltpu.VMEM((ng, S, LANES), F32)],
        compiler_params=_params(("parallel", "parallel", "arbitrary"), 48),
    )(proj3, proj3, proj3, cos_t, sin_t)


def _rope_tables(S):
    half = ROPE_DIM // 2
    inv = ROPE_THETA ** (-jnp.arange(half, dtype=F32) / half)
    ang = jnp.arange(S, dtype=F32)[:, None] * inv[None, :]
    cos, sin = jnp.cos(ang), jnp.sin(ang)
    pad = ATT_HEAD_DIM - ROPE_DIM
    cos_h = jnp.concatenate([cos, cos, jnp.ones((S, pad), F32)], axis=1)
    sin_h = jnp.concatenate([sin, sin, jnp.zeros((S, pad), F32)], axis=1)
    tile = lambda t: jnp.concatenate([t, t], axis=1)
    return tile(cos_h), tile(sin_h)


DN_BLK = 128
DN_UNROLL = 8
DN_DOUBLINGS = 5


def _dn_kernel(q_ref, k_ref, v_ref, z_ref, sm_ref, cwq_ref, cwk_ref, cwv_ref, par_ref, ng_ref, o_ref,
               xp, qn, kn, vn, kq_s, b_s, dec_s, oacc, *, S):
    h = pl.program_id(1)
    rc = 256
    halo = 8

    parts = ((q_ref, cwq_ref, qn, "q"), (k_ref, cwk_ref, kn, "k"), (v_ref, cwv_ref, vn, "v"))
    for a in range(len(parts)):
        xp[a, 0:halo, :] = jnp.zeros((halo, LANES), F32)
        xp[a, S + halo:S + 2 * halo, :] = jnp.zeros((halo, LANES), F32)

    def cp(c, carry):
        r0 = pl.multiple_of(c * rc, rc)
        for a, (src, _, _, _) in enumerate(parts):
            xp[a, pl.ds(r0 + halo, rc), :] = src[0, pl.ds(r0, rc), :].astype(F32)
        return carry

    lax.fori_loop(0, S // rc, cp, 0)

    def cv(c, carry):
        r0 = pl.multiple_of(c * rc, rc)
        off = halo - SHORT_CONV // 2
        for a, (_, cw, dst, mode) in enumerate(parts):
            y = xp[a, pl.ds(r0 + off, rc), :] * cw[0:1, :]
            for i in range(1, SHORT_CONV):
                y = y + xp[a, pl.ds(r0 + off + i, rc), :] * cw[i:i + 1, :]
            y = y * jax.nn.sigmoid(y)
            if mode != "v":
                y = y * lax.rsqrt(jnp.sum(y * y, axis=-1, keepdims=True) + EPS)
            if mode == "q":
                y = y * (DN_HEAD_DIM ** -0.5)
            dst[pl.ds(r0, rc), :] = y
        return carry

    lax.fori_loop(0, S // rc, cv, 0, unroll=2)

    lane = lax.broadcasted_iota(jnp.int32, (DN_BLK, DN_BLK), 1)
    rowi = lax.broadcasted_iota(jnp.int32, (DN_BLK, DN_BLK), 0)
    eye = jnp.where(lane == rowi, 1.0, 0.0)
    same_half = (rowi >= DN_BLK // 2) == (lane >= DN_BLK // 2)
    a_exp = jnp.exp(par_ref[0:1, :])
    dt_bias = par_ref[1:2, :]
    n_blocks = S // DN_BLK

    def masks(d):
        return (lane <= rowi, lane < rowi) if d == 0 else (lane >= rowi, lane > rowi)

    tri3 = [jnp.concatenate([jnp.where(masks(d)[0], 1.0, 0.0).astype(BF16)] * 3, axis=1) for d in range(2)]

    def prepass(c, carry):
        chains = []
        for uu in range(DN_UNROLL):
            blk = c * DN_UNROLL + uu
            rows = pl.ds(pl.multiple_of(blk * DN_BLK, DN_BLK), DN_BLK)
            sm = sm_ref[0, rows, :]
            beta_all = jax.nn.sigmoid(sm)
            g_all = -a_exp * jax.nn.softplus(sm + dt_bias)
            q = qn[rows, :]
            k = kn[rows, :]
            v = vn[rows, :]
            for d in range(2):
                beta = jnp.sum(jnp.where(lane == h + DN_HEADS * d, beta_all, 0.0), axis=1, keepdims=True)
                g = jnp.sum(jnp.where(lane == h + DN_HEADS * (2 + d), g_all, 0.0), axis=1, keepdims=True)
                gb = jnp.broadcast_to(g, (DN_BLK, DN_BLK))
                hi = gb.astype(BF16)
                r1 = gb - hi.astype(F32)
                mid = r1.astype(BF16)
                lo = (r1 - mid.astype(F32)).astype(BF16)
                chains.append(dict(blk=blk, rows=rows, d=d, q=q, k=k, kb=k * beta, vb=v * beta,
                                   g3=jnp.concatenate([hi, mid, lo], axis=0)))
            fwd, bwd = chains[-2], chains[-1]
            kk = _mm_nt(jnp.concatenate([fwd["kb"], bwd["kb"], q], axis=0), k)
            fwd["kk"], bwd["kk"] = kk[:DN_BLK], kk[DN_BLK:2 * DN_BLK]
            fwd["qk"] = bwd["qk"] = kk[2 * DN_BLK:]
        for ch in chains:
            d = ch["d"]
            tri = masks(d)[0]
            gc = jnp.dot(tri3[d], ch["g3"], preferred_element_type=F32)
            last = DN_BLK - 1 if d == 0 else 0
            ch["gc"], ch["gl"] = gc, gc[last:last + 1, :]
            ch["dm"] = jnp.where(tri, jnp.exp(jnp.where(tri, gc - gc.T, 0.0)), 0.0)
        for ch in chains:
            nm = jnp.where(masks(ch["d"])[1], -(ch["kk"] * ch["dm"]), 0.0)
            ch["nd"] = jnp.where(same_half, nm, 0.0)
            ch["no"] = jnp.where(same_half, 0.0, nm)
        for ch in chains:
            ch["x"] = eye + ch["nd"]
            ch["p"] = _mm(ch["nd"], ch["nd"])
        for _ in range(DN_DOUBLINGS - 1):
            for ch in chains:
                r = _mm(jnp.concatenate([ch["x"], ch["p"]], axis=0), ch["p"])
                ch["x"] = ch["x"] + r[:DN_BLK]
                ch["p"] = r[DN_BLK:]
        for ch in chains:
            ch["x"] = ch["x"] + _mm(ch["x"], ch["p"])
            ch["y"] = _mm(ch["no"], ch["x"])
        for ch in chains:
            eg = jnp.exp(ch["gc"])
            x = ch["x"] + _mm(ch["x"], ch["y"])
            ch["wu"] = _mm(x, jnp.concatenate([ch["kb"] * eg, ch["vb"]], axis=1))
            ch["qg"] = ch["q"] * eg
            ch["kg"] = ch["k"] * jnp.exp(ch["gl"] - ch["gc"])
        for ch in chains:
            iw = _mm(ch["qk"] * ch["dm"], ch["wu"])
            ch["qp"] = ch["qg"] - iw[:, :LANES]
            ch["oc"] = iw[:, LANES:]
        for ch in chains:
            d, blk = ch["d"], ch["blk"]
            kb_ = _mm_tn(ch["kg"], ch["wu"])
            base = pl.multiple_of(blk * 2 * DN_BLK, 2 * DN_BLK)
            kq_s[d, pl.ds(base, DN_BLK), :] = (-kb_[:, :LANES]).astype(BF16)
            kq_s[d, pl.ds(base + DN_BLK, DN_BLK), :] = ch["qp"].astype(BF16)
            b_s[d, ch["rows"], :] = kb_[:, LANES:]
            dec_s[d, pl.ds(pl.multiple_of(blk * 8, 8), 8), :] = jnp.broadcast_to(jnp.exp(ch["gl"]), (8, LANES))
        for uu in range(DN_UNROLL):
            oacc[chains[2 * uu]["rows"], :] = chains[2 * uu]["oc"] + chains[2 * uu + 1]["oc"]
        return carry

    lax.fori_loop(0, n_blocks // DN_UNROLL, prepass, 0)

    def step(d, blk, state):
        base = pl.multiple_of(blk * 2 * DN_BLK, 2 * DN_BLK)
        r = jnp.dot(kq_s[d, pl.ds(base, 2 * DN_BLK), :], state.astype(BF16), preferred_element_type=F32)
        rows = pl.ds(pl.multiple_of(blk * DN_BLK, DN_BLK), DN_BLK)
        oacc[rows, :] += r[DN_BLK:]
        dec = dec_s[d, pl.ds(pl.multiple_of(blk * 8, 8), 1), :]
        return state * dec + r[:DN_BLK] + b_s[d, rows, :]

    def rec(t, carry):
        sf, sb = carry
        bf = 2 * t
        bb = n_blocks - 1 - 2 * t
        sf = step(0, bf, sf)
        sb = step(1, bb, sb)
        sf = step(0, bf + 1, sf)
        sb = step(1, bb - 1, sb)
        return sf, sb

    zero_state = jnp.zeros((DN_HEAD_DIM, DN_HEAD_DIM), F32)
    lax.fori_loop(0, n_blocks // 2, rec, (zero_state, zero_state))

    def fin(c, carry):
        sl = pl.ds(pl.multiple_of(c * rc, rc), rc)
        o = oacc[sl, :]
        o = o * lax.rsqrt(jnp.mean(o * o, axis=-1, keepdims=True) + EPS) * ng_ref[...]
        z = z_ref[0, sl, :].astype(F32)
        o_ref[0, sl, :] = (o * (z * jax.nn.sigmoid(z))).astype(o_ref.dtype)
        return carry

    lax.fori_loop(0, S // rc, fin, 0, unroll=2)


def _deltanet(proj3, small3, conv_w, par, norm_g):
    B, S, _ = proj3.shape

    def col_spec(col0):
        base = col0 // LANES
        return pl.BlockSpec((1, S, LANES), lambda b, h: (b, 0, base + h))

    def conv_spec(part):
        return pl.BlockSpec((SHORT_CONV, LANES), lambda b, h: (0, part * DN_HEADS + h))

    seq_f32 = pltpu.VMEM((S, LANES), F32)
    return pl.pallas_call(
        functools.partial(_dn_kernel, S=S),
        grid=(B, DN_HEADS),
        in_specs=[col_spec(COL_DN_Q), col_spec(COL_DN_K), col_spec(COL_DN_V), col_spec(COL_Z),
                  pl.BlockSpec((1, S, LANES), lambda b, h: (b, 0, 0)),
                  conv_spec(0), conv_spec(1), conv_spec(2),
                  pl.BlockSpec((8, LANES), lambda b, h: (0, 0)),
                  pl.BlockSpec((1, LANES), lambda b, h: (0, 0))],
        out_specs=pl.BlockSpec((1, S, LANES), lambda b, h: (b, 0, h)),
        out_shape=jax.ShapeDtypeStruct((B, S, DN_WIDTH), BF16),
        scratch_shapes=[pltpu.VMEM((3, S + 16, LANES), F32), seq_f32, seq_f32, seq_f32,
                        pltpu.VMEM((2, 2 * S, LANES), BF16),
                        pltpu.VMEM((2, S, LANES), F32),
                        pltpu.VMEM((2, S // DN_BLK * 8, LANES), F32),
                        seq_f32],
        compiler_params=_params(("parallel", "parallel"), 56),
    )(proj3, proj3, proj3, proj3, small3, conv_w, conv_w, conv_w, par, norm_g.reshape(1, LANES))


def _mix_kernel(x_ref, oa_ref, ob_ref, ga_ref, gb_ref, wa_ref, wb_ref, wo_ref, o_ref):
    ya = jnp.dot(oa_ref[...], wa_ref[...], preferred_element_type=F32)
    yb = jnp.dot(ob_ref[...], wb_ref[...], preferred_element_type=F32)
    mix = jax.nn.sigmoid(ga_ref[...].astype(F32)) * ya + jax.nn.sigmoid(gb_ref[...].astype(F32)) * yb
    o_ref[...] = x_ref[...] + jnp.dot(mix.astype(BF16), wo_ref[...], preferred_element_type=F32)


def _mix(x2d, o_att, o_dn, proj, wa, wb, wo, *, tm):
    T = x2d.shape[0]
    row = lambda width, cb=0: pl.BlockSpec((tm, width), lambda i: (i, cb))
    full = lambda a: pl.BlockSpec(a.shape, lambda i: (0, 0))
    return pl.pallas_call(
        _mix_kernel,
        grid=(T // tm,),
        in_specs=[row(D_MODEL), row(ATT_GROUP_WIDTH), row(DN_WIDTH),
                  row(D_MODEL, COL_GATE_A // D_MODEL), row(D_MODEL, COL_GATE_B // D_MODEL),
                  full(wa), full(wb), full(wo)],
        out_specs=row(D_MODEL),
        out_shape=jax.ShapeDtypeStruct((T, D_MODEL), F32),
        compiler_params=_params(("parallel",), 48),
    )(x2d, o_att, o_dn, proj, proj, wa, wb, wo)


FFN_KC = 256
FFN_HALO = 16


def _ffn_out_kernel(g_ref, v_ref, gp_ref, gn_ref, x_ref, cw_ref, cb_ref, wd_ref, nf_ref, o_ref, act_ref,
                    *, tiles_per_seq):
    i = pl.program_id(0)
    tm = g_ref.shape[0]
    pos = i % tiles_per_seq
    keep_prev = jnp.where(pos == 0, 0.0, 1.0)
    keep_next = jnp.where(pos == tiles_per_seq - 1, 0.0, 1.0)
    row = lax.broadcasted_iota(jnp.int32, (tm, FFN_KC), 0)
    for c in range(D_FF // FFN_KC):
        sl = slice(c * FFN_KC, (c + 1) * FFN_KC)
        g = g_ref[:, sl].astype(F32)
        prev_row = gp_ref[FFN_HALO - 1:FFN_HALO, sl].astype(F32) * keep_prev
        next_row = gn_ref[0:1, sl].astype(F32) * keep_next
        g_prev = jnp.where(row == 0, prev_row, pltpu.roll(g, 1, 0))
        g_next = jnp.where(row == tm - 1, next_row, pltpu.roll(g, tm - 1, 0))
        conv = g_prev * cw_ref[0:1, sl] + g * cw_ref[1:2, sl] + g_next * cw_ref[2:3, sl] + cb_ref[:, sl]
        gelu = 0.5 * conv * (1.0 + lax.erf(conv * (2.0 ** -0.5)))
        act_ref[:, sl] = (gelu * v_ref[:, sl].astype(F32)).astype(BF16)
    x2 = x_ref[...] + jnp.dot(act_ref[...], wd_ref[...], preferred_element_type=F32)
    ms = jnp.mean(x2 * x2, axis=-1, keepdims=True)
    o_ref[...] = x2 * lax.rsqrt(ms + EPS) * nf_ref[...]


def _ffn_out(up, x1, conv_w, conv_b, wd, norm_g, *, tm, S):
    T = x1.shape[0]
    tiles_per_seq = S // tm
    hb = tm // FFN_HALO
    last_hb = T // FFN_HALO - 1
    return pl.pallas_call(
        functools.partial(_ffn_out_kernel, tiles_per_seq=tiles_per_seq),
        grid=(T // tm,),
        in_specs=[
            pl.BlockSpec((tm, D_FF), lambda i: (i, 0)),
            pl.BlockSpec((tm, D_FF), lambda i: (i, 1)),
            pl.BlockSpec((FFN_HALO, D_FF), lambda i: (jnp.maximum(i * hb - 1, 0), 0)),
            pl.BlockSpec((FFN_HALO, D_FF), lambda i: (jnp.minimum((i + 1) * hb, last_hb), 0)),
            pl.BlockSpec((tm, D_MODEL), lambda i: (i, 0)),
            pl.BlockSpec((3, D_FF), lambda i: (0, 0)),
            pl.BlockSpec((1, D_FF), lambda i: (0, 0)),
            pl.BlockSpec((D_FF, D_MODEL), lambda i: (0, 0)),
            pl.BlockSpec((1, D_MODEL), lambda i: (0, 0)),
        ],
        out_specs=pl.BlockSpec((tm, D_MODEL), lambda i: (i, 0)),
        out_shape=jax.ShapeDtypeStruct((T, D_MODEL), F32),
        scratch_shapes=[pltpu.VMEM((tm, D_FF), BF16)],
        compiler_params=_params(("parallel",), 56),
    )(up, up, up, up, x1, conv_w, conv_b.reshape(1, D_FF), wd, norm_g.reshape(1, D_MODEL))


def _trunk(x, w):
    B, S, _ = x.shape
    T = B * S
    x2d = x.reshape(T, D_MODEL)
    proj, small = _norm_matmul(x2d, w["norm_mix_g"], w["w_in"], w["w_small"], tm=1024, tn=1536)
    proj3 = proj.reshape(B, S, PROJ_COLS)
    o_att = _attention(proj3, *_rope_tables(S))
    o_dn = _deltanet(proj3, small.reshape(B, S, LANES), w["conv_qkv_w"], w["dn_par"], w["out_norm_g"])
    x1 = _mix(x2d, o_att.reshape(T, ATT_GROUP_WIDTH), o_dn.reshape(T, DN_WIDTH), proj,
              w["w_branch_a"], w["w_branch_b"], w["w_out"], tm=512)
    up = _norm_matmul(x1, w["norm_ffn_g"], w["w_up"], tm=1024, tn=D_FF)[0]
    y = _ffn_out(up, x1, w["ffn_conv_w"], w["ffn_conv_b"], w["w_down"], w["norm_final_g"], tm=512, S=S)
    return y.reshape(B, S, D_MODEL)


def kernel(x_prompt, x_sample, norm_mix_g, w_in, conv_qkv_w, a_log_f, a_log_b, dt_bias_f, dt_bias_b, out_norm_g, w_branch_a, w_branch_b, w_out, norm_ffn_g, w_up, ffn_conv_w, ffn_conv_b, w_down, norm_final_g):
    att = 3 * ATT_WIDTH
    dn_end = att + 3 * DN_WIDTH
    z_end = dn_end + DN_WIDTH
    small_end = z_end + 4 * DN_HEADS
    w_in_r = jnp.concatenate([
        w_in[:, small_end:],
        w_in[:, dn_end:z_end],
        w_in[:, att:dn_end],
        w_in[:, :ATT_WIDTH] * (ATT_HEAD_DIM ** -0.5),
        w_in[:, ATT_WIDTH:att],
    ], axis=1).astype(BF16)
    w_small = jnp.pad(w_in[:, z_end:small_end], ((0, 0), (0, LANES - 4 * DN_HEADS))).astype(BF16)
    par = jnp.zeros((8, LANES), F32)
    par = par.at[0, 2 * DN_HEADS:4 * DN_HEADS].set(jnp.concatenate([a_log_f, a_log_b]))
    par = par.at[1, 2 * DN_HEADS:4 * DN_HEADS].set(jnp.concatenate([dt_bias_f, dt_bias_b]))
    w = dict(
        norm_mix_g=norm_mix_g, w_in=w_in_r, w_small=w_small, conv_qkv_w=conv_qkv_w, dn_par=par, out_norm_g=out_norm_g,
        w_branch_a=w_branch_a.astype(BF16), w_branch_b=w_branch_b.astype(BF16), w_out=w_out.astype(BF16),
        norm_ffn_g=norm_ffn_g, w_up=w_up.astype(BF16), ffn_conv_w=ffn_conv_w, ffn_conv_b=ffn_conv_b,
        w_down=w_down.astype(BF16), norm_final_g=norm_final_g,
    )
    return _trunk(x_prompt, w), _trunk(x_sample, w)
```

```python
import functools

import jax
import jax.numpy as jnp
from jax import lax
from jax.experimental import pallas as pl
from jax.experimental.pallas import tpu as pltpu

F32 = jnp.float32
BF16 = jnp.bfloat16

D_MODEL = 1024
ATT_HEAD_DIM = 64
ATT_HEADS_PER_GROUP = 8
DILATIONS = (1, 4, 16)
ATT_HALF = 64
ATT_GROUP_WIDTH = ATT_HEADS_PER_GROUP * ATT_HEAD_DIM
ATT_WIDTH = len(DILATIONS) * ATT_GROUP_WIDTH
ROPE_DIM = ATT_HEAD_DIM // 4
ROPE_THETA = 500000.0
DN_HEADS = 8
DN_HEAD_DIM = 128
DN_WIDTH = DN_HEADS * DN_HEAD_DIM
SHORT_CONV = 5
D_FF = 2816
EPS = 1e-6
NEG_INF = -1e30
LOG2_E = 1.4426950408889634

LANES = 128
MIB = 1024 * 1024

COL_GATE_A = 0
COL_GATE_B = 1024
COL_Z = 2048
COL_DN_Q = 3072
COL_DN_K = 4096
COL_DN_V = 5120
COL_ATT_Q = 6144
COL_ATT_K = COL_ATT_Q + ATT_WIDTH
COL_ATT_V = COL_ATT_K + ATT_WIDTH
PROJ_COLS = COL_ATT_V + ATT_WIDTH


def _params(semantics, vmem_mib):
    return pltpu.CompilerParams(dimension_semantics=semantics, vmem_limit_bytes=vmem_mib * MIB)


def _mm(a, b):
    return jnp.dot(a.astype(BF16), b.astype(BF16), preferred_element_type=F32)


def _mm_nt(a, b):
    return lax.dot_general(a.astype(BF16), b.astype(BF16), (((1,), (1,)), ((), ())),
                           preferred_element_type=F32)


def _mm_tn(a, b):
    return lax.dot_general(a.astype(BF16), b.astype(BF16), (((0,), (0,)), ((), ())),
                           preferred_element_type=F32)


def _norm_matmul_kernel(*refs, row_chunk, with_side):
    if with_side:
        x_ref, g_ref, w_ref, ws_ref, o_ref, os_ref, h_ref = refs
    else:
        x_ref, g_ref, w_ref, o_ref, h_ref = refs

    @pl.when(pl.program_id(1) == 0)
    def _():
        def body(c, carry):
            r0 = pl.multiple_of(c * row_chunk, row_chunk)
            x = x_ref[pl.ds(r0, row_chunk), :]
            ms = jnp.mean(x * x, axis=-1, keepdims=True)
            h_ref[pl.ds(r0, row_chunk), :] = (x * lax.rsqrt(ms + EPS) * g_ref[...]).astype(BF16)
            return carry
        lax.fori_loop(0, x_ref.shape[0] // row_chunk, body, 0, unroll=2)
        if with_side:
            os_ref[...] = jnp.dot(h_ref[...], ws_ref[...], preferred_element_type=F32)

    o_ref[...] = jnp.dot(h_ref[...], w_ref[...], preferred_element_type=F32).astype(o_ref.dtype)


def _norm_matmul(x2d, gain, w_bf16, w_side=None, *, tm, tn):
    T, K = x2d.shape
    N = w_bf16.shape[1]
    with_side = w_side is not None
    in_specs = [
        pl.BlockSpec((tm, K), lambda i, j: (i, 0)),
        pl.BlockSpec((1, K), lambda i, j: (0, 0)),
        pl.BlockSpec((K, tn), lambda i, j: (0, j)),
    ]
    out_specs = [pl.BlockSpec((tm, tn), lambda i, j: (i, j))]
    out_shape = [jax.ShapeDtypeStruct((T, N), BF16)]
    args = [x2d, gain.reshape(1, K), w_bf16]
    if with_side:
        ns = w_side.shape[1]
        in_specs.append(pl.BlockSpec((K, ns), lambda i, j: (0, 0)))
        out_specs.append(pl.BlockSpec((tm, ns), lambda i, j: (i, 0)))
        out_shape.append(jax.ShapeDtypeStruct((T, ns), F32))
        args.append(w_side)
    return pl.pallas_call(
        functools.partial(_norm_matmul_kernel, row_chunk=128, with_side=with_side),
        grid=(T // tm, N // tn),
        in_specs=in_specs,
        out_specs=out_specs,
        out_shape=out_shape,
        scratch_shapes=[pltpu.VMEM((tm, K), BF16)],
        compiler_params=_params(("parallel", "arbitrary"), 48),
    )(*args)


ATT_QB = 128
ATT_UNROLL = 8


def _attn_group(q_ref, k_ref, v_ref, cos_ref, sin_ref, x32, bias_ref, qs, ks, vs, og_ref, lse_ref,
                *, S, dil, gi):
    L = S // dil
    rc = min(L, 256)
    nlc = L // rc

    def rows_of(start, n):
        return pl.ds(start, n, stride=dil) if dil > 1 else pl.ds(pl.multiple_of(start, ATT_QB), n)

    if dil > 1:
        def widen(c, carry):
            sl = pl.ds(pl.multiple_of(c * 256, 256), 256)
            for a, ref in enumerate((q_ref, k_ref, v_ref)):
                x32[a, sl, :] = ref[0, sl, :].astype(F32)
            return carry

        lax.fori_loop(0, S // 256, widen, 0)
        load = lambda a, rows: x32[a, rows, :]
    else:
        load = lambda a, rows: (q_ref, k_ref, v_ref)[a][0, rows, :].astype(F32)

    lane_in = lax.broadcasted_iota(jnp.int32, (LANES, LANES), 0)
    lane_out = lax.broadcasted_iota(jnp.int32, (LANES, LANES), 1)
    head_lane = lane_out % ATT_HEAD_DIM
    half = ROPE_DIM // 2
    rot_m = jnp.where((head_lane < half) & (lane_in == lane_out + half), -1.0,
                      jnp.where((head_lane >= half) & (head_lane < ROPE_DIM) & (lane_in == lane_out - half),
                                1.0, 0.0)).astype(BF16)

    def pre(idx, carry):
        r = idx // nlc if dil > 1 else 0
        lc = idx % nlc
        rows = rows_of(r + dil * lc * rc, rc)
        pos = r * L + lc * rc
        cs = cos_ref[rows, :]
        sn = sin_ref[rows, :]

        def rope(t):
            return t * cs + jnp.dot(t.astype(BF16), rot_m, preferred_element_type=F32) * sn

        q = rope(load(0, rows)) * LOG2_E
        ks[pl.ds(pl.multiple_of(pos, rc), rc), :] = rope(load(1, rows)).astype(BF16)
        vs[pl.ds(pl.multiple_of(pos, rc), rc), :] = load(2, rows).astype(BF16)
        head_a = lax.broadcasted_iota(jnp.int32, (rc, LANES), 1) < ATT_HEAD_DIM
        qa = jnp.where(head_a, q, 0.0).astype(BF16)
        qb = jnp.where(head_a, 0.0, q).astype(BF16)
        for sub in range(rc // ATT_QB):
            dst = pl.multiple_of(2 * pos + 2 * sub * ATT_QB, 2 * ATT_QB)
            qs[pl.ds(dst, ATT_QB), :] = qa[sub * ATT_QB:(sub + 1) * ATT_QB]
            qs[pl.ds(dst + ATT_QB, ATT_QB), :] = qb[sub * ATT_QB:(sub + 1) * ATT_QB]
        return carry

    lax.fori_loop(0, dil * nlc, pre, 0, unroll=2 * 256 // rc)

    nb = L // ATT_QB
    nk = min(2 * ATT_QB, L)
    col = lax.broadcasted_iota(jnp.int32, (ATT_QB, nk), 1)
    row = lax.broadcasted_iota(jnp.int32, (ATT_QB, nk), 0)
    head_a = lax.broadcasted_iota(jnp.int32, (ATT_QB, LANES), 1) < ATT_HEAD_DIM
    for v in range(3):
        band = jnp.where(jnp.abs(col - row - v * ATT_HALF) <= ATT_HALF, 0.0, NEG_INF)
        bias_ref[v, 0:ATT_QB, 0:nk] = band
        bias_ref[v, ATT_QB:2 * ATT_QB, 0:nk] = band

    def both(t):
        return jnp.where(head_a, t[:ATT_QB], t[ATT_QB:])

    def blk(it, carry):
        work = []
        for ub in range(ATT_UNROLL):
            idx = it * ATT_UNROLL + ub
            r = idx // nb if dil > 1 else 0
            n = idx % nb
            q0 = n * ATT_QB
            start = jnp.clip(q0 - ATT_HALF, 0, L - nk)
            qsl = pl.ds(pl.multiple_of(2 * (r * L + q0), 2 * ATT_QB), 2 * ATT_QB)
            ksl = pl.ds(pl.multiple_of(r * L + start, ATT_HALF), nk)
            s = lax.dot_general(qs[qsl, :], ks[ksl, :], (((1,), (1,)), ((), ())), preferred_element_type=F32)
            work.append(dict(rows=rows_of(r + dil * q0, ATT_QB), ksl=ksl, s=s,
                             variant=(q0 - start) // ATT_HALF))
        for w in work:
            s = w["s"] + bias_ref[w["variant"], :, 0:nk]
            w["m"] = jnp.max(s, axis=1, keepdims=True)
            p = jnp.exp2(s - w["m"])
            w["l"] = jnp.sum(p, axis=1, keepdims=True)
            w["p"] = p.astype(BF16)
        for w in work:
            w["pv"] = jnp.dot(w["p"], vs[w["ksl"], :], preferred_element_type=F32)
        for w in work:
            l = both(w["l"])
            og_ref[gi, w["rows"], :] = both(w["pv"]) / l
            lse_ref[gi, w["rows"], :] = both(w["m"]) + jnp.log2(l)
        return carry

    lax.fori_loop(0, dil * nb // ATT_UNROLL, blk, 0)


def _attn_kernel(q_ref, k_ref, v_ref, cos_ref, sin_ref, o_ref,
                 x32, bias_ref, qs, ks, vs, og_ref, lse_ref, *, S):
    g = pl.program_id(2)
    for gi, dil in enumerate(DILATIONS):
        @pl.when(g == gi)
        def _(gi=gi, dil=dil):
            _attn_group(q_ref, k_ref, v_ref, cos_ref, sin_ref, x32, bias_ref, qs, ks, vs, og_ref, lse_ref,
                        S=S, dil=dil, gi=gi)

    @pl.when(g == len(DILATIONS) - 1)
    def _():
        rc = 256

        def fin(c, carry):
            sl = pl.ds(pl.multiple_of(c * rc, rc), rc)
            lse = [lse_ref[gi, sl, :] for gi in range(len(DILATIONS))]
            top = functools.reduce(jnp.maximum, lse)
            wts = [jnp.exp2(t - top) for t in lse]
            num = sum(wt * og_ref[gi, sl, :] for gi, wt in enumerate(wts))
            o_ref[0, sl, :] = (num / sum(wts)).astype(o_ref.dtype)
            return carry

        lax.fori_loop(0, S // rc, fin, 0, unroll=2)


def _attention(proj3, cos_t, sin_t):
    B, S, _ = proj3.shape
    ng = len(DILATIONS)

    def col_spec(col0):
        base = col0 // LANES
        per_group = ATT_GROUP_WIDTH // LANES
        return pl.BlockSpec((1, S, LANES), lambda b, j, g: (b, 0, base + g * per_group + j))

    tab_spec = pl.BlockSpec((S, LANES), lambda b, j, g: (0, 0))
    seq_bf16 = pltpu.VMEM((S, LANES), BF16)
    return pl.pallas_call(
        functools.partial(_attn_kernel, S=S),
        grid=(B, ATT_GROUP_WIDTH // LANES, ng),
        in_specs=[col_spec(COL_ATT_Q), col_spec(COL_ATT_K), col_spec(COL_ATT_V), tab_spec, tab_spec],
        out_specs=pl.BlockSpec((1, S, LANES), lambda b, j, g: (b, 0, j)),
        out_shape=jax.ShapeDtypeStruct((B, S, ATT_GROUP_WIDTH), BF16),
        scratch_shapes=[pltpu.VMEM((3, S, LANES), F32),
                        pltpu.VMEM((3, 2 * ATT_QB, 2 * ATT_QB), F32),
                        pltpu.VMEM((2 * S, LANES), BF16), seq_bf16, seq_bf16,
                        pltpu.VMEM((ng, S, LANES), F32), pltpu.VMEM((ng, S, LANES), F32)],
        compiler_params=_params(("parallel", "parallel", "arbitrary"), 48),
    )(proj3, proj3, proj3, cos_t, sin_t)


def _rope_tables(S):
    half = ROPE_DIM // 2
    inv = ROPE_THETA ** (-jnp.arange(half, dtype=F32) / half)
    ang = jnp.arange(S, dtype=F32)[:, None] * inv[None, :]
    cos, sin = jnp.cos(ang), jnp.sin(ang)
    pad = ATT_HEAD_DIM - ROPE_DIM
    cos_h = jnp.concatenate([cos, cos, jnp.ones((S, pad), F32)], axis=1)
    sin_h = jnp.concatenate([sin, sin, jnp.zeros((S, pad), F32)], axis=1)
    tile = lambda t: jnp.concatenate([t, t], axis=1)
    return tile(cos_h), tile(sin_h)


DN_BLK = 128
DN_UNROLL = 8
DN_DOUBLINGS = 5


def _dn_kernel(q_ref, k_ref, v_ref, z_ref, sm_ref, cwq_ref, cwk_ref, cwv_ref, par_ref, ng_ref, o_ref,
               xp, qn, kn, vn, kq_s, b_s, dec_s, oacc, *, S):
    h = pl.program_id(1)
    rc = 256
    halo = 8

    parts = ((q_ref, cwq_ref, qn, "q"), (k_ref, cwk_ref, kn, "k"), (v_ref, cwv_ref, vn, "v"))
    for a in range(len(parts)):
        xp[a, 0:halo, :] = jnp.zeros((halo, LANES), F32)
        xp[a, S + halo:S + 2 * halo, :] = jnp.zeros((halo, LANES), F32)

    def cp(c, carry):
        r0 = pl.multiple_of(c * rc, rc)
        for a, (src, _, _, _) in enumerate(parts):
            xp[a, pl.ds(r0 + halo, rc), :] = src[0, pl.ds(r0, rc), :].astype(F32)
        return carry

    lax.fori_loop(0, S // rc, cp, 0)

    def cv(c, carry):
        r0 = pl.multiple_of(c * rc, rc)
        off = halo - SHORT_CONV // 2
        for a, (_, cw, dst, mode) in enumerate(parts):
            y = xp[a, pl.ds(r0 + off, rc), :] * cw[0:1, :]
            for i in range(1, SHORT_CONV):
                y = y + xp[a, pl.ds(r0 + off + i, rc), :] * cw[i:i + 1, :]
            y = y * jax.nn.sigmoid(y)
            if mode != "v":
                y = y * lax.rsqrt(jnp.sum(y * y, axis=-1, keepdims=True) + EPS)
            if mode == "q":
                y = y * (DN_HEAD_DIM ** -0.5)
            dst[pl.ds(r0, rc), :] = y
        return carry

    lax.fori_loop(0, S // rc, cv, 0, unroll=2)

    lane = lax.broadcasted_iota(jnp.int32, (DN_BLK, DN_BLK), 1)
    rowi = lax.broadcasted_iota(jnp.int32, (DN_BLK, DN_BLK), 0)
    eye = jnp.where(lane == rowi, 1.0, 0.0)
    same_half = (rowi >= DN_BLK // 2) == (lane >= DN_BLK // 2)
    a_exp = jnp.exp(par_ref[0:1, :])
    dt_bias = par_ref[1:2, :]
    n_blocks = S // DN_BLK

    def masks(d):
        return (lane <= rowi, lane < rowi) if d == 0 else (lane >= rowi, lane > rowi)

    tri3 = [jnp.concatenate([jnp.where(masks(d)[0], 1.0, 0.0).astype(BF16)] * 3, axis=1) for d in range(2)]

    def prepass(c, carry):
        chains = []
        for uu in range(DN_UNROLL):
            blk = c * DN_UNROLL + uu
            rows = pl.ds(pl.multiple_of(blk * DN_BLK, DN_BLK), DN_BLK)
            sm = sm_ref[0, rows, :]
            beta_all = jax.nn.sigmoid(sm)
            g_all = -a_exp * jax.nn.softplus(sm + dt_bias)
            q = qn[rows, :]
            k = kn[rows, :]
            v = vn[rows, :]
            for d in range(2):
                beta = jnp.sum(jnp.where(lane == h + DN_HEADS * d, beta_all, 0.0), axis=1, keepdims=True)
                g = jnp.sum(jnp.where(lane == h + DN_HEADS * (2 + d), g_all, 0.0), axis=1, keepdims=True)
                gb = jnp.broadcast_to(g, (DN_BLK, DN_BLK))
                hi = gb.astype(BF16)
                r1 = gb - hi.astype(F32)
                mid = r1.astype(BF16)
                lo = (r1 - mid.astype(F32)).astype(BF16)
                chains.append(dict(blk=blk, rows=rows, d=d, q=q, k=k, kb=k * beta, vb=v * beta,
                                   g3=jnp.concatenate([hi, mid, lo], axis=0)))
            fwd, bwd = chains[-2], chains[-1]
            kk = _mm_nt(jnp.concatenate([fwd["kb"], bwd["kb"], q], axis=0), k)
            fwd["kk"], bwd["kk"] = kk[:DN_BLK], kk[DN_BLK:2 * DN_BLK]
            fwd["qk"] = bwd["qk"] = kk[2 * DN_BLK:]
        for ch in chains:
            d = ch["d"]
            tri = masks(d)[0]
            gc = jnp.dot(tri3[d], ch["g3"], preferred_element_type=F32)
            last = DN_BLK - 1 if d == 0 else 0
            ch["gc"], ch["gl"] = gc, gc[last:last + 1, :]
            ch["dm"] = jnp.where(tri, jnp.exp(jnp.where(tri, gc - gc.T, 0.0)), 0.0)
        for ch in chains:
            nm = jnp.where(masks(ch["d"])[1], -(ch["kk"] * ch["dm"]), 0.0)
            ch["nd"] = jnp.where(same_half, nm, 0.0)
            ch["no"] = jnp.where(same_half, 0.0, nm)
        for ch in chains:
            ch["x"] = eye + ch["nd"]
            ch["p"] = _mm(ch["nd"], ch["nd"])
        for _ in range(DN_DOUBLINGS - 1):
            for ch in chains:
                r = _mm(jnp.concatenate([ch["x"], ch["p"]], axis=0), ch["p"])
                ch["x"] = ch["x"] + r[:DN_BLK]
                ch["p"] = r[DN_BLK:]
        for ch in chains:
            ch["x"] = ch["x"] + _mm(ch["x"], ch["p"])
            ch["y"] = _mm(ch["no"], ch["x"])
        for ch in chains:
            eg = jnp.exp(ch["gc"])
            x = ch["x"] + _mm(ch["x"], ch["y"])
            ch["wu"] = _mm(x, jnp.concatenate([ch["kb"] * eg, ch["vb"]], axis=1))
            ch["qg"] = ch["q"] * eg
            ch["kg"] = ch["k"] * jnp.exp(ch["gl"] - ch["gc"])
        for ch in chains:
            iw = _mm(ch["qk"] * ch["dm"], ch["wu"])
            ch["qp"] = ch["qg"] - iw[:, :LANES]
            ch["oc"] = iw[:, LANES:]
        for ch in chains:
            d, blk = ch["d"], ch["blk"]
            kb_ = _mm_tn(ch["kg"], ch["wu"])
            base = pl.multiple_of(blk * 2 * DN_BLK, 2 * DN_BLK)
            kq_s[d, pl.ds(base, DN_BLK), :] = (-kb_[:, :LANES]).astype(BF16)
            kq_s[d, pl.ds(base + DN_BLK, DN_BLK), :] = ch["qp"].astype(BF16)
            b_s[d, ch["rows"], :] = kb_[:, LANES:]
            dec_s[d, pl.ds(pl.multiple_of(blk * 8, 8), 8), :] = jnp.broadcast_to(jnp.exp(ch["gl"]), (8, LANES))
        for uu in range(DN_UNROLL):
            oacc[chains[2 * uu]["rows"], :] = chains[2 * uu]["oc"] + chains[2 * uu + 1]["oc"]
        return carry

    lax.fori_loop(0, n_blocks // DN_UNROLL, prepass, 0)

    def step(d, blk, state):
        base = pl.multiple_of(blk * 2 * DN_BLK, 2 * DN_BLK)
        r = jnp.dot(kq_s[d, pl.ds(base, 2 * DN_BLK), :], state.astype(BF16), preferred_element_type=F32)
        rows = pl.ds(pl.multiple_of(blk * DN_BLK, DN_BLK), DN_BLK)
        oacc[rows, :] += r[DN_BLK:]
        dec = dec_s[d, pl.ds(pl.multiple_of(blk * 8, 8), 1), :]
        return state * dec + r[:DN_BLK] + b_s[d, rows, :]

    def rec(t, carry):
        sf, sb = carry
        bf = 2 * t
        bb = n_blocks - 1 - 2 * t
        sf = step(0, bf, sf)
        sb = step(1, bb, sb)
        sf = step(0, bf + 1, sf)
        sb = step(1, bb - 1, sb)
        return sf, sb

    zero_state = jnp.zeros((DN_HEAD_DIM, DN_HEAD_DIM), F32)
    lax.fori_loop(0, n_blocks // 2, rec, (zero_state, zero_state))

    def fin(c, carry):
        sl = pl.ds(pl.multiple_of(c * rc, rc), rc)
        o = oacc[sl, :]
        o = o * lax.rsqrt(jnp.mean(o * o, axis=-1, keepdims=True) + EPS) * ng_ref[...]
        z = z_ref[0, sl, :].astype(F32)
        o_ref[0, sl, :] = (o * (z * jax.nn.sigmoid(z))).astype(o_ref.dtype)
        return carry

    lax.fori_loop(0, S // rc, fin, 0, unroll=2)


def _deltanet(proj3, small3, conv_w, par, norm_g):
    B, S, _ = proj3.shape

    def col_spec(col0):
        base = col0 // LANES
        return pl.BlockSpec((1, S, LANES), lambda b, h: (b, 0, base + h))

    def conv_spec(part):
        return pl.BlockSpec((SHORT_CONV, LANES), lambda b, h: (0, part * DN_HEADS + h))

    seq_f32 = pltpu.VMEM((S, LANES), F32)
    return pl.pallas_call(
        functools.partial(_dn_kernel, S=S),
        grid=(B, DN_HEADS),
        in_specs=[col_spec(COL_DN_Q), col_spec(COL_DN_K), col_spec(COL_DN_V), col_spec(COL_Z),
                  pl.BlockSpec((1, S, LANES), lambda b, h: (b, 0, 0)),
                  conv_spec(0), conv_spec(1), conv_spec(2),
                  pl.BlockSpec((8, LANES), lambda b, h: (0, 0)),
                  pl.BlockSpec((1, LANES), lambda b, h: (0, 0))],
        out_specs=pl.BlockSpec((1, S, LANES), lambda b, h: (b, 0, h)),
        out_shape=jax.ShapeDtypeStruct((B, S, DN_WIDTH), BF16),
        scratch_shapes=[pltpu.VMEM((3, S + 16, LANES), F32), seq_f32, seq_f32, seq_f32,
                        pltpu.VMEM((2, 2 * S, LANES), BF16),
                        pltpu.VMEM((2, S, LANES), F32),
                        pltpu.VMEM((2, S // DN_BLK * 8, LANES), F32),
                        seq_f32],
        compiler_params=_params(("parallel", "parallel"), 56),
    )(proj3, proj3, proj3, proj3, small3, conv_w, conv_w, conv_w, par, norm_g.reshape(1, LANES))


def _mix_kernel(x_ref, oa_ref, ob_ref, ga_ref, gb_ref, wa_ref, wb_ref, wo_ref, o_ref):
    ya = jnp.dot(oa_ref[...], wa_ref[...], preferred_element_type=F32)
    yb = jnp.dot(ob_ref[...], wb_ref[...], preferred_element_type=F32)
    mix = jax.nn.sigmoid(ga_ref[...].astype(F32)) * ya + jax.nn.sigmoid(gb_ref[...].astype(F32)) * yb
    o_ref[...] = x_ref[...] + jnp.dot(mix.astype(BF16), wo_ref[...], preferred_element_type=F32)


def _mix(x2d, o_att, o_dn, proj, wa, wb, wo, *, tm):
    T = x2d.shape[0]
    row = lambda width, cb=0: pl.BlockSpec((tm, width), lambda i: (i, cb))
    full = lambda a: pl.BlockSpec(a.shape, lambda i: (0, 0))
    return pl.pallas_call(
        _mix_kernel,
        grid=(T // tm,),
        in_specs=[row(D_MODEL), row(ATT_GROUP_WIDTH), row(DN_WIDTH),
                  row(D_MODEL, COL_GATE_A // D_MODEL), row(D_MODEL, COL_GATE_B // D_MODEL),
                  full(wa), full(wb), full(wo)],
        out_specs=row(D_MODEL),
        out_shape=jax.ShapeDtypeStruct((T, D_MODEL), F32),
        compiler_params=_params(("parallel",), 48),
    )(x2d, o_att, o_dn, proj, proj, wa, wb, wo)


FFN_KC = 256
FFN_HALO = 16


def _ffn_out_kernel(g_ref, v_ref, gp_ref, gn_ref, x_ref, cw_ref, cb_ref, wd_ref, nf_ref, o_ref, act_ref,
                    *, tiles_per_seq):
    i = pl.program_id(0)
    tm = g_ref.shape[0]
    pos = i % tiles_per_seq
    keep_prev = jnp.where(pos == 0, 0.0, 1.0)
    keep_next = jnp.where(pos == tiles_per_seq - 1, 0.0, 1.0)
    row = lax.broadcasted_iota(jnp.int32, (tm, FFN_KC), 0)
    for c in range(D_FF // FFN_KC):
        sl = slice(c * FFN_KC, (c + 1) * FFN_KC)
        g = g_ref[:, sl].astype(F32)
        prev_row = gp_ref[FFN_HALO - 1:FFN_HALO, sl].astype(F32) * keep_prev
        next_row = gn_ref[0:1, sl].astype(F32) * keep_next
        g_prev = jnp.where(row == 0, prev_row, pltpu.roll(g, 1, 0))
        g_next = jnp.where(row == tm - 1, next_row, pltpu.roll(g, tm - 1, 0))
        conv = g_prev * cw_ref[0:1, sl] + g * cw_ref[1:2, sl] + g_next * cw_ref[2:3, sl] + cb_ref[:, sl]
        gelu = 0.5 * conv * (1.0 + lax.erf(conv * (2.0 ** -0.5)))
        act_ref[:, sl] = (gelu * v_ref[:, sl].astype(F32)).astype(BF16)
    x2 = x_ref[...] + jnp.dot(act_ref[...], wd_ref[...], preferred_element_type=F32)
    ms = jnp.mean(x2 * x2, axis=-1, keepdims=True)
    o_ref[...] = x2 * lax.rsqrt(ms + EPS) * nf_ref[...]


def _ffn_out(up, x1, conv_w, conv_b, wd, norm_g, *, tm, S):
    T = x1.shape[0]
    tiles_per_seq = S // tm
    hb = tm // FFN_HALO
    last_hb = T // FFN_HALO - 1
    return pl.pallas_call(
        functools.partial(_ffn_out_kernel, tiles_per_seq=tiles_per_seq),
        grid=(T // tm,),
        in_specs=[
            pl.BlockSpec((tm, D_FF), lambda i: (i, 0)),
            pl.BlockSpec((tm, D_FF), lambda i: (i, 1)),
            pl.BlockSpec((FFN_HALO, D_FF), lambda i: (jnp.maximum(i * hb - 1, 0), 0)),
            pl.BlockSpec((FFN_HALO, D_FF), lambda i: (jnp.minimum((i + 1) * hb, last_hb), 0)),
            pl.BlockSpec((tm, D_MODEL), lambda i: (i, 0)),
            pl.BlockSpec((3, D_FF), lambda i: (0, 0)),
            pl.BlockSpec((1, D_FF), lambda i: (0, 0)),
            pl.BlockSpec((D_FF, D_MODEL), lambda i: (0, 0)),
            pl.BlockSpec((1, D_MODEL), lambda i: (0, 0)),
        ],
        out_specs=pl.BlockSpec((tm, D_MODEL), lambda i: (i, 0)),
        out_shape=jax.ShapeDtypeStruct((T, D_MODEL), F32),
        scratch_shapes=[pltpu.VMEM((tm, D_FF), BF16)],
        compiler_params=_params(("parallel",), 56),
    )(up, up, up, up, x1, conv_w, conv_b.reshape(1, D_FF), wd, norm_g.reshape(1, D_MODEL))


def _trunk(x, w):
    B, S, _ = x.shape
    T = B * S
    x2d = x.reshape(T, D_MODEL)
    proj, small = _norm_matmul(x2d, w["norm_mix_g"], w["w_in"], w["w_small"], tm=1024, tn=1536)
    proj3 = proj.reshape(B, S, PROJ_COLS)
    o_att = _attention(proj3, *_rope_tables(S))
    o_dn = _deltanet(proj3, small.reshape(B, S, LANES), w["conv_qkv_w"], w["dn_par"], w["out_norm_g"])
    x1 = _mix(x2d, o_att.reshape(T, ATT_GROUP_WIDTH), o_dn.reshape(T, DN_WIDTH), proj,
              w["w_branch_a"], w["w_branch_b"], w["w_out"], tm=512)
    up = _norm_matmul(x1, w["norm_ffn_g"], w["w_up"], tm=1024, tn=D_FF)[0]
    y = _ffn_out(up, x1, w["ffn_conv_w"], w["ffn_conv_b"], w["w_down"], w["norm_final_g"], tm=512, S=S)
    return y.reshape(B, S, D_MODEL)


def kernel(x_prompt, x_sample, norm_mix_g, w_in, conv_qkv_w, a_log_f, a_log_b, dt_bias_f, dt_bias_b, out_norm_g, w_branch_a, w_branch_b, w_out, norm_ffn_g, w_up, ffn_conv_w, ffn_conv_b, w_down, norm_final_g):
    att = 3 * ATT_WIDTH
    dn_end = att + 3 * DN_WIDTH
    z_end = dn_end + DN_WIDTH
    small_end = z_end + 4 * DN_HEADS
    w_in_r = jnp.concatenate([
        w_in[:, small_end:],
        w_in[:, dn_end:z_end],
        w_in[:, att:dn_end],
        w_in[:, :ATT_WIDTH] * (ATT_HEAD_DIM ** -0.5),
        w_in[:, ATT_WIDTH:att],
    ], axis=1).astype(BF16)
    w_small = jnp.pad(w_in[:, z_end:small_end], ((0, 0), (0, LANES - 4 * DN_HEADS))).astype(BF16)
    par = jnp.zeros((8, LANES), F32)
    par = par.at[0, 2 * DN_HEADS:4 * DN_HEADS].set(jnp.concatenate([a_log_f, a_log_b]))
    par = par.at[1, 2 * DN_HEADS:4 * DN_HEADS].set(jnp.concatenate([dt_bias_f, dt_bias_b]))
    w = dict(
        norm_mix_g=norm_mix_g, w_in=w_in_r, w_small=w_small, conv_qkv_w=conv_qkv_w, dn_par=par, out_norm_g=out_norm_g,
        w_branch_a=w_branch_a.astype(BF16), w_branch_b=w_branch_b.astype(BF16), w_out=w_out.astype(BF16),
        norm_ffn_g=norm_ffn_g, w_up=w_up.astype(BF16), ffn_conv_w=ffn_conv_w, ffn_conv_b=ffn_conv_b,
        w_down=w_down.astype(BF16), norm_final_g=norm_final_g,
    )
    return _trunk(x_prompt, w), _trunk(x_sample, w)
```

```python
import functools

import jax
import jax.numpy as jnp
from jax import lax
from jax.experimental import pallas as pl
from jax.experimental.pallas import tpu as pltpu

F32 = jnp.float32
BF16 = jnp.bfloat16

D_MODEL = 1024
ATT_HEAD_DIM = 64
ATT_HEADS_PER_GROUP = 8
DILATIONS = (1, 4, 16)
ATT_HALF = 64
ATT_GROUP_WIDTH = ATT_HEADS_PER_GROUP * ATT_HEAD_DIM
ATT_WIDTH = len(DILATIONS) * ATT_GROUP_WIDTH
ROPE_DIM = ATT_HEAD_DIM // 4
ROPE_THETA = 500000.0
DN_HEADS = 8
DN_HEAD_DIM = 128
DN_WIDTH = DN_HEADS * DN_HEAD_DIM
SHORT_CONV = 5
D_FF = 2816
EPS = 1e-6
NEG_INF = -1e30
LOG2_E = 1.4426950408889634

LANES = 128
MIB = 1024 * 1024

COL_GATE_A = 0
COL_GATE_B = 1024
COL_Z = 2048
COL_DN_Q = 3072
COL_DN_K = 4096
COL_DN_V = 5120
COL_ATT_Q = 6144
COL_ATT_K = COL_ATT_Q + ATT_WIDTH
COL_ATT_V = COL_ATT_K + ATT_WIDTH
PROJ_COLS = COL_ATT_V + ATT_WIDTH


def _params(semantics, vmem_mib):
    return pltpu.CompilerParams(dimension_semantics=semantics, vmem_limit_bytes=vmem_mib * MIB)


def _mm(a, b):
    return jnp.dot(a.astype(BF16), b.astype(BF16), preferred_element_type=F32)


def _mm_nt(a, b):
    return lax.dot_general(a.astype(BF16), b.astype(BF16), (((1,), (1,)), ((), ())),
                           preferred_element_type=F32)


def _mm_tn(a, b):
    return lax.dot_general(a.astype(BF16), b.astype(BF16), (((0,), (0,)), ((), ())),
                           preferred_element_type=F32)


def _norm_matmul_kernel(*refs, row_chunk, with_side):
    if with_side:
        x_ref, g_ref, w_ref, ws_ref, o_ref, os_ref, h_ref = refs
    else:
        x_ref, g_ref, w_ref, o_ref, h_ref = refs

    @pl.when(pl.program_id(1) == 0)
    def _():
        def body(c, carry):
            r0 = pl.multiple_of(c * row_chunk, row_chunk)
            x = x_ref[pl.ds(r0, row_chunk), :]
            ms = jnp.mean(x * x, axis=-1, keepdims=True)
            h_ref[pl.ds(r0, row_chunk), :] = (x * lax.rsqrt(ms + EPS) * g_ref[...]).astype(BF16)
            return carry
        lax.fori_loop(0, x_ref.shape[0] // row_chunk, body, 0, unroll=2)
        if with_side:
            os_ref[...] = jnp.dot(h_ref[...], ws_ref[...], preferred_element_type=F32)

    o_ref[...] = jnp.dot(h_ref[...], w_ref[...], preferred_element_type=F32).astype(o_ref.dtype)


def _norm_matmul(x2d, gain, w_bf16, w_side=None, *, tm, tn):
    T, K = x2d.shape
    N = w_bf16.shape[1]
    with_side = w_side is not None
    in_specs = [
        pl.BlockSpec((tm, K), lambda i, j: (i, 0)),
        pl.BlockSpec((1, K), lambda i, j: (0, 0)),
        pl.BlockSpec((K, tn), lambda i, j: (0, j)),
    ]
    out_specs = [pl.BlockSpec((tm, tn), lambda i, j: (i, j))]
    out_shape = [jax.ShapeDtypeStruct((T, N), BF16)]
    args = [x2d, gain.reshape(1, K), w_bf16]
    if with_side:
        ns = w_side.shape[1]
        in_specs.append(pl.BlockSpec((K, ns), lambda i, j: (0, 0)))
        out_specs.append(pl.BlockSpec((tm, ns), lambda i, j: (i, 0)))
        out_shape.append(jax.ShapeDtypeStruct((T, ns), F32))
        args.append(w_side)
    return pl.pallas_call(
        functools.partial(_norm_matmul_kernel, row_chunk=128, with_side=with_side),
        grid=(T // tm, N // tn),
        in_specs=in_specs,
        out_specs=out_specs,
        out_shape=out_shape,
        scratch_shapes=[pltpu.VMEM((tm, K), BF16)],
        compiler_params=_params(("parallel", "arbitrary"), 48),
    )(*args)


ATT_QB = 128
ATT_UNROLL = 8


def _attn_group(q_ref, k_ref, v_ref, cos_ref, sin_ref, x32, bias_ref, qs, ks, vs, og_ref, lse_ref,
                *, S, dil, gi):
    L = S // dil
    rc = min(L, 256)
    nlc = L // rc

    def rows_of(start, n):
        return pl.ds(start, n, stride=dil) if dil > 1 else pl.ds(pl.multiple_of(start, ATT_QB), n)

    if dil > 1:
        def widen(c, carry):
            sl = pl.ds(pl.multiple_of(c * 256, 256), 256)
            for a, ref in enumerate((q_ref, k_ref, v_ref)):
                x32[a, sl, :] = ref[0, sl, :].astype(F32)
            return carry

        lax.fori_loop(0, S // 256, widen, 0)
        load = lambda a, rows: x32[a, rows, :]
    else:
        load = lambda a, rows: (q_ref, k_ref, v_ref)[a][0, rows, :].astype(F32)

    lane_in = lax.broadcasted_iota(jnp.int32, (LANES, LANES), 0)
    lane_out = lax.broadcasted_iota(jnp.int32, (LANES, LANES), 1)
    head_lane = lane_out % ATT_HEAD_DIM
    half = ROPE_DIM // 2
    rot_m = jnp.where((head_lane < half) & (lane_in == lane_out + half), -1.0,
                      jnp.where((head_lane >= half) & (head_lane < ROPE_DIM) & (lane_in == lane_out - half),
                                1.0, 0.0)).astype(BF16)

    def pre(idx, carry):
        r = idx // nlc if dil > 1 else 0
        lc = idx % nlc
        rows = rows_of(r + dil * lc * rc, rc)
        pos = r * L + lc * rc
        cs = cos_ref[rows, :]
        sn = sin_ref[rows, :]

        def rope(t):
            return t * cs + jnp.dot(t.astype(BF16), rot_m, preferred_element_type=F32) * sn

        q = rope(load(0, rows)) * LOG2_E
        ks[pl.ds(pl.multiple_of(pos, rc), rc), :] = rope(load(1, rows)).astype(BF16)
        vs[pl.ds(pl.multiple_of(pos, rc), rc), :] = load(2, rows).astype(BF16)
        head_a = lax.broadcasted_iota(jnp.int32, (rc, LANES), 1) < ATT_HEAD_DIM
        qa = jnp.where(head_a, q, 0.0).astype(BF16)
        qb = jnp.where(head_a, 0.0, q).astype(BF16)
        for sub in range(rc // ATT_QB):
            dst = pl.multiple_of(2 * pos + 2 * sub * ATT_QB, 2 * ATT_QB)
            qs[pl.ds(dst, ATT_QB), :] = qa[sub * ATT_QB:(sub + 1) * ATT_QB]
            qs[pl.ds(dst + ATT_QB, ATT_QB), :] = qb[sub * ATT_QB:(sub + 1) * ATT_QB]
        return carry

    lax.fori_loop(0, dil * nlc, pre, 0, unroll=2 * 256 // rc)

    nb = L // ATT_QB
    nk = min(2 * ATT_QB, L)
    col = lax.broadcasted_iota(jnp.int32, (ATT_QB, nk), 1)
    row = lax.broadcasted_iota(jnp.int32, (ATT_QB, nk), 0)
    head_a = lax.broadcasted_iota(jnp.int32, (ATT_QB, LANES), 1) < ATT_HEAD_DIM
    for v in range(3):
        band = jnp.where(jnp.abs(col - row - v * ATT_HALF) <= ATT_HALF, 0.0, NEG_INF)
        bias_ref[v, 0:ATT_QB, 0:nk] = band
        bias_ref[v, ATT_QB:2 * ATT_QB, 0:nk] = band

    def both(t):
        return jnp.where(head_a, t[:ATT_QB], t[ATT_QB:])

    def blk(it, carry):
        work = []
        for ub in range(ATT_UNROLL):
            idx = it * ATT_UNROLL + ub
            r = idx // nb if dil > 1 else 0
            n = idx % nb
            q0 = n * ATT_QB
            start = jnp.clip(q0 - ATT_HALF, 0, L - nk)
            qsl = pl.ds(pl.multiple_of(2 * (r * L + q0), 2 * ATT_QB), 2 * ATT_QB)
            ksl = pl.ds(pl.multiple_of(r * L + start, ATT_HALF), nk)
            s = lax.dot_general(qs[qsl, :], ks[ksl, :], (((1,), (1,)), ((), ())), preferred_element_type=F32)
            work.append(dict(rows=rows_of(r + dil * q0, ATT_QB), ksl=ksl, s=s,
                             variant=(q0 - start) // ATT_HALF))
        for w in work:
            s = w["s"] + bias_ref[w["variant"], :, 0:nk]
            w["m"] = jnp.max(s, axis=1, keepdims=True)
            p = jnp.exp2(s - w["m"])
            w["l"] = jnp.sum(p, axis=1, keepdims=True)
            w["p"] = p.astype(BF16)
        for w in work:
            w["pv"] = jnp.dot(w["p"], vs[w["ksl"], :], preferred_element_type=F32)
        for w in work:
            l = both(w["l"])
            og_ref[gi, w["rows"], :] = both(w["pv"]) / l
            lse_ref[gi, w["rows"], :] = both(w["m"]) + jnp.log2(l)
        return carry

    lax.fori_loop(0, dil * nb // ATT_UNROLL, blk, 0)


def _attn_kernel(q_ref, k_ref, v_ref, cos_ref, sin_ref, o_ref,
                 x32, bias_ref, qs, ks, vs, og_ref, lse_ref, *, S):
    g = pl.program_id(2)
    for gi, dil in enumerate(DILATIONS):
        @pl.when(g == gi)
        def _(gi=gi, dil=dil):
            _attn_group(q_ref, k_ref, v_ref, cos_ref, sin_ref, x32, bias_ref, qs, ks, vs, og_ref, lse_ref,
                        S=S, dil=dil, gi=gi)

    @pl.when(g == len(DILATIONS) - 1)
    def _():
        rc = 256

        def fin(c, carry):
            sl = pl.ds(pl.multiple_of(c * rc, rc), rc)
            lse = [lse_ref[gi, sl, :] for gi in range(len(DILATIONS))]
            top = functools.reduce(jnp.maximum, lse)
            wts = [jnp.exp2(t - top) for t in lse]
            num = sum(wt * og_ref[gi, sl, :] for gi, wt in enumerate(wts))
            o_ref[0, sl, :] = (num / sum(wts)).astype(o_ref.dtype)
            return carry

        lax.fori_loop(0, S // rc, fin, 0, unroll=2)


def _attention(proj3, cos_t, sin_t):
    B, S, _ = proj3.shape
    ng = len(DILATIONS)

    def col_spec(col0):
        base = col0 // LANES
        per_group = ATT_GROUP_WIDTH // LANES
        return pl.BlockSpec((1, S, LANES), lambda b, j, g: (b, 0, base + g * per_group + j))

    tab_spec = pl.BlockSpec((S, LANES), lambda b, j, g: (0, 0))
    seq_bf16 = pltpu.VMEM((S, LANES), BF16)
    return pl.pallas_call(
        functools.partial(_attn_kernel, S=S),
        grid=(B, ATT_GROUP_WIDTH // LANES, ng),
        in_specs=[col_spec(COL_ATT_Q), col_spec(COL_ATT_K), col_spec(COL_ATT_V), tab_spec, tab_spec],
        out_specs=pl.BlockSpec((1, S, LANES), lambda b, j, g: (b, 0, j)),
        out_shape=jax.ShapeDtypeStruct((B, S, ATT_GROUP_WIDTH), BF16),
        scratch_shapes=[pltpu.VMEM((3, S, LANES), F32),
                        pltpu.VMEM((3, 2 * ATT_QB, 2 * ATT_QB), F32),
                        pltpu.VMEM((2 * S, LANES), BF16), seq_bf16, seq_bf16,
                        pltpu.VMEM((ng, S, LANES), F32), pltpu.VMEM((ng, S, LANES), F32)],
        compiler_params=_params(("parallel", "parallel", "arbitrary"), 48),
    )(proj3, proj3, proj3, cos_t, sin_t)


def _rope_tables(S):
    half = ROPE_DIM // 2
    inv = ROPE_THETA ** (-jnp.arange(half, dtype=F32) / half)
    ang = jnp.arange(S, dtype=F32)[:, None] * inv[None, :]
    cos, sin = jnp.cos(ang), jnp.sin(ang)
    pad = ATT_HEAD_DIM - ROPE_DIM
    cos_h = jnp.concatenate([cos, cos, jnp.ones((S, pad), F32)], axis=1)
    sin_h = jnp.concatenate([sin, sin, jnp.zeros((S, pad), F32)], axis=1)
    tile = lambda t: jnp.concatenate([t, t], axis=1)
    return tile(cos_h), tile(sin_h)


DN_BLK = 128
DN_UNROLL = 8
DN_DOUBLINGS = 5


def _dn_kernel(q_ref, k_ref, v_ref, z_ref, sm_ref, cwq_ref, cwk_ref, cwv_ref, par_ref, ng_ref, o_ref,
               xp, qn, kn, vn, kq_s, b_s, dec_s, oacc, *, S):
    h = pl.program_id(1)
    rc = 256
    halo = 8

    parts = ((q_ref, cwq_ref, qn, "q"), (k_ref, cwk_ref, kn, "k"), (v_ref, cwv_ref, vn, "v"))
    for a in range(len(parts)):
        xp[a, 0:halo, :] = jnp.zeros((halo, LANES), F32)
        xp[a, S + halo:S + 2 * halo, :] = jnp.zeros((halo, LANES), F32)

    def cp(c, carry):
        r0 = pl.multiple_of(c * rc, rc)
        for a, (src, _, _, _) in enumerate(parts):
            xp[a, pl.ds(r0 + halo, rc), :] = src[0, pl.ds(r0, rc), :].astype(F32)
        return carry

    lax.fori_loop(0, S // rc, cp, 0)

    def cv(c, carry):
        r0 = pl.multiple_of(c * rc, rc)
        off = halo - SHORT_CONV // 2
        for a, (_, cw, dst, mode) in enumerate(parts):
            y = xp[a, pl.ds(r0 + off, rc), :] * cw[0:1, :]
            for i in range(1, SHORT_CONV):
                y = y + xp[a, pl.ds(r0 + off + i, rc), :] * cw[i:i + 1, :]
            y = y * jax.nn.sigmoid(y)
            if mode != "v":
                y = y * lax.rsqrt(jnp.sum(y * y, axis=-1, keepdims=True) + EPS)
            if mode == "q":
                y = y * (DN_HEAD_DIM ** -0.5)
            dst[pl.ds(r0, rc), :] = y
        return carry

    lax.fori_loop(0, S // rc, cv, 0, unroll=2)

    lane = lax.broadcasted_iota(jnp.int32, (DN_BLK, DN_BLK), 1)
    rowi = lax.broadcasted_iota(jnp.int32, (DN_BLK, DN_BLK), 0)
    eye = jnp.where(lane == rowi, 1.0, 0.0)
    same_half = (rowi >= DN_BLK // 2) == (lane >= DN_BLK // 2)
    a_exp = jnp.exp(par_ref[0:1, :])
    dt_bias = par_ref[1:2, :]
    n_blocks = S // DN_BLK

    def masks(d):
        return (lane <= rowi, lane < rowi) if d == 0 else (lane >= rowi, lane > rowi)

    tri3 = [jnp.concatenate([jnp.where(masks(d)[0], 1.0, 0.0).astype(BF16)] * 3, axis=1) for d in range(2)]

    def prepass(c, carry):
        chains = []
        for uu in range(DN_UNROLL):
            blk = c * DN_UNROLL + uu
            rows = pl.ds(pl.multiple_of(blk * DN_BLK, DN_BLK), DN_BLK)
            sm = sm_ref[0, rows, :]
            beta_all = jax.nn.sigmoid(sm)
            g_all = -a_exp * jax.nn.softplus(sm + dt_bias)
            q = qn[rows, :]
            k = kn[rows, :]
            v = vn[rows, :]
            for d in range(2):
                beta = jnp.sum(jnp.where(lane == h + DN_HEADS * d, beta_all, 0.0), axis=1, keepdims=True)
                g = jnp.sum(jnp.where(lane == h + DN_HEADS * (2 + d), g_all, 0.0), axis=1, keepdims=True)
                gb = jnp.broadcast_to(g, (DN_BLK, DN_BLK))
                hi = gb.astype(BF16)
                r1 = gb - hi.astype(F32)
                mid = r1.astype(BF16)
                lo = (r1 - mid.astype(F32)).astype(BF16)
                chains.append(dict(blk=blk, rows=rows, d=d, q=q, k=k, kb=k * beta, vb=v * beta,
                                   g3=jnp.concatenate([hi, mid, lo], axis=0)))
            fwd, bwd = chains[-2], chains[-1]
            kk = _mm_nt(jnp.concatenate([fwd["kb"], bwd["kb"], q], axis=0), k)
            fwd["kk"], bwd["kk"] = kk[:DN_BLK], kk[DN_BLK:2 * DN_BLK]
            fwd["qk"] = bwd["qk"] = kk[2 * DN_BLK:]
        for ch in chains:
            d = ch["d"]
            tri = masks(d)[0]
            gc = jnp.dot(tri3[d], ch["g3"], preferred_element_type=F32)
            last = DN_BLK - 1 if d == 0 else 0
            ch["gc"], ch["gl"] = gc, gc[last:last + 1, :]
            ch["dm"] = jnp.where(tri, jnp.exp(jnp.where(tri, gc - gc.T, 0.0)), 0.0)
        for ch in chains:
            nm = jnp.where(masks(ch["d"])[1], -(ch["kk"] * ch["dm"]), 0.0)
            ch["nd"] = jnp.where(same_half, nm, 0.0)
            ch["no"] = jnp.where(same_half, 0.0, nm)
        for ch in chains:
            ch["x"] = eye + ch["nd"]
            ch["p"] = _mm(ch["nd"], ch["nd"])
        for _ in range(DN_DOUBLINGS - 1):
            for ch in chains:
                r = _mm(jnp.concatenate([ch["x"], ch["p"]], axis=0), ch["p"])
                ch["x"] = ch["x"] + r[:DN_BLK]
                ch["p"] = r[DN_BLK:]
        for ch in chains:
            ch["x"] = ch["x"] + _mm(ch["x"], ch["p"])
            ch["y"] = _mm(ch["no"], ch["x"])
        for ch in chains:
            eg = jnp.exp(ch["gc"])
            x = ch["x"] + _mm(ch["x"], ch["y"])
            ch["wu"] = _mm(x, jnp.concatenate([ch["kb"] * eg, ch["vb"]], axis=1))
            ch["qg"] = ch["q"] * eg
            ch["kg"] = ch["k"] * jnp.exp(ch["gl"] - ch["gc"])
        for ch in chains:
            iw = _mm(ch["qk"] * ch["dm"], ch["wu"])
            ch["qp"] = ch["qg"] - iw[:, :LANES]
            ch["oc"] = iw[:, LANES:]
        for ch in chains:
            d, blk = ch["d"], ch["blk"]
            kb_ = _mm_tn(ch["kg"], ch["wu"])
            base = pl.multiple_of(blk * 2 * DN_BLK, 2 * DN_BLK)
            kq_s[d, pl.ds(base, DN_BLK), :] = (-kb_[:, :LANES]).astype(BF16)
            kq_s[d, pl.ds(base + DN_BLK, DN_BLK), :] = ch["qp"].astype(BF16)
            b_s[d, ch["rows"], :] = kb_[:, LANES:]
            dec_s[d, pl.ds(pl.multiple_of(blk * 8, 8), 8), :] = jnp.broadcast_to(jnp.exp(ch["gl"]), (8, LANES))
        for uu in range(DN_UNROLL):
            oacc[chains[2 * uu]["rows"], :] = chains[2 * uu]["oc"] + chains[2 * uu + 1]["oc"]
        return carry

    lax.fori_loop(0, n_blocks // DN_UNROLL, prepass, 0)

    def step(d, blk, state):
        base = pl.multiple_of(blk * 2 * DN_BLK, 2 * DN_BLK)
        r = jnp.dot(kq_s[d, pl.ds(base, 2 * DN_BLK), :], state.astype(BF16), preferred_element_type=F32)
        rows = pl.ds(pl.multiple_of(blk * DN_BLK, DN_BLK), DN_BLK)
        oacc[rows, :] += r[DN_BLK:]
        dec = dec_s[d, pl.ds(pl.multiple_of(blk * 8, 8), 1), :]
        return state * dec + r[:DN_BLK] + b_s[d, rows, :]

    def rec(t, carry):
        sf, sb = carry
        bf = 2 * t
        bb = n_blocks - 1 - 2 * t
        sf = step(0, bf, sf)
        sb = step(1, bb, sb)
        sf = step(0, bf + 1, sf)
        sb = step(1, bb - 1, sb)
        return sf, sb

    zero_state = jnp.zeros((DN_HEAD_DIM, DN_HEAD_DIM), F32)
    lax.fori_loop(0, n_blocks // 2, rec, (zero_state, zero_state))

    def fin(c, carry):
        sl = pl.ds(pl.multiple_of(c * rc, rc), rc)
        o = oacc[sl, :]
        o = o * lax.rsqrt(jnp.mean(o * o, axis=-1, keepdims=True) + EPS) * ng_ref[...]
        z = z_ref[0, sl, :].astype(F32)
        o_ref[0, sl, :] = (o * (z * jax.nn.sigmoid(z))).astype(o_ref.dtype)
        return carry

    lax.fori_loop(0, S // rc, fin, 0, unroll=2)


def _deltanet(proj3, small3, conv_w, par, norm_g):
    B, S, _ = proj3.shape

    def col_spec(col0):
        base = col0 // LANES
        return pl.BlockSpec((1, S, LANES), lambda b, h: (b, 0, base + h))

    def conv_spec(part):
        return pl.BlockSpec((SHORT_CONV, LANES), lambda b, h: (0, part * DN_HEADS + h))

    seq_f32 = pltpu.VMEM((S, LANES), F32)
    return pl.pallas_call(
        functools.partial(_dn_kernel, S=S),
        grid=(B, DN_HEADS),
        in_specs=[col_spec(COL_DN_Q), col_spec(COL_DN_K), col_spec(COL_DN_V), col_spec(COL_Z),
                  pl.BlockSpec((1, S, LANES), lambda b, h: (b, 0, 0)),
                  conv_spec(0), conv_spec(1), conv_spec(2),
                  pl.BlockSpec((8, LANES), lambda b, h: (0, 0)),
                  pl.BlockSpec((1, LANES), lambda b, h: (0, 0))],
        out_specs=pl.BlockSpec((1, S, LANES), lambda b, h: (b, 0, h)),
        out_shape=jax.ShapeDtypeStruct((B, S, DN_WIDTH), BF16),
        scratch_shapes=[pltpu.VMEM((3, S + 16, LANES), F32), seq_f32, seq_f32, seq_f32,
                        pltpu.VMEM((2, 2 * S, LANES), BF16),
                        pltpu.VMEM((2, S, LANES), F32),
                        pltpu.VMEM((2, S // DN_BLK * 8, LANES), F32),
                        seq_f32],
        compiler_params=_params(("parallel", "parallel"), 56),
    )(proj3, proj3, proj3, proj3, small3, conv_w, conv_w, conv_w, par, norm_g.reshape(1, LANES))


def _mix_kernel(x_ref, oa_ref, ob_ref, ga_ref, gb_ref, wa_ref, wb_ref, wo_ref, o_ref):
    ya = jnp.dot(oa_ref[...], wa_ref[...], preferred_element_type=F32)
    yb = jnp.dot(ob_ref[...], wb_ref[...], preferred_element_type=F32)
    mix = jax.nn.sigmoid(ga_ref[...].astype(F32)) * ya + jax.nn.sigmoid(gb_ref[...].astype(F32)) * yb
    o_ref[...] = x_ref[...] + jnp.dot(mix.astype(BF16), wo_ref[...], preferred_element_type=F32)


def _mix(x2d, o_att, o_dn, proj, wa, wb, wo, *, tm):
    T = x2d.shape[0]
    row = lambda width, cb=0: pl.BlockSpec((tm, width), lambda i: (i, cb))
    stream = row
    full = lambda a: pl.BlockSpec(a.shape, lambda i: (0, 0), pipeline_mode=pl.Buffered(1))
    return pl.pallas_call(
        _mix_kernel,
        grid=(T // tm,),
        in_specs=[stream(D_MODEL), stream(ATT_GROUP_WIDTH), stream(DN_WIDTH),
                  stream(D_MODEL, COL_GATE_A // D_MODEL), stream(D_MODEL, COL_GATE_B // D_MODEL),
                  full(wa), full(wb), full(wo)],
        out_specs=row(D_MODEL),
        out_shape=jax.ShapeDtypeStruct((T, D_MODEL), F32),
        compiler_params=_params(("parallel",), 56),
    )(x2d, o_att, o_dn, proj, proj, wa, wb, wo)


FFN_KC = 256
FFN_HALO = 16


def _ffn_out_kernel(g_ref, v_ref, gp_ref, gn_ref, x_ref, cw_ref, cb_ref, wd_ref, nf_ref, o_ref, act_ref,
                    *, tiles_per_seq):
    i = pl.program_id(0)
    tm = g_ref.shape[0]
    pos = i % tiles_per_seq
    keep_prev = jnp.where(pos == 0, 0.0, 1.0)
    keep_next = jnp.where(pos == tiles_per_seq - 1, 0.0, 1.0)
    row = lax.broadcasted_iota(jnp.int32, (tm, FFN_KC), 0)
    for c in range(D_FF // FFN_KC):
        sl = slice(c * FFN_KC, (c + 1) * FFN_KC)
        g = g_ref[:, sl].astype(F32)
        prev_row = gp_ref[FFN_HALO - 1:FFN_HALO, sl].astype(F32) * keep_prev
        next_row = gn_ref[0:1, sl].astype(F32) * keep_next
        g_prev = jnp.where(row == 0, prev_row, pltpu.roll(g, 1, 0))
        g_next = jnp.where(row == tm - 1, next_row, pltpu.roll(g, tm - 1, 0))
        conv = g_prev * cw_ref[0:1, sl] + g * cw_ref[1:2, sl] + g_next * cw_ref[2:3, sl] + cb_ref[:, sl]
        gelu = 0.5 * conv * (1.0 + lax.erf(conv * (2.0 ** -0.5)))
        act_ref[:, sl] = (gelu * v_ref[:, sl].astype(F32)).astype(BF16)
    x2 = x_ref[...] + jnp.dot(act_ref[...], wd_ref[...], preferred_element_type=F32)
    ms = jnp.mean(x2 * x2, axis=-1, keepdims=True)
    o_ref[...] = x2 * lax.rsqrt(ms + EPS) * nf_ref[...]


def _ffn_out(up, x1, conv_w, conv_b, wd, norm_g, *, tm, S):
    T = x1.shape[0]
    tiles_per_seq = S // tm
    hb = tm // FFN_HALO
    last_hb = T // FFN_HALO - 1
    return pl.pallas_call(
        functools.partial(_ffn_out_kernel, tiles_per_seq=tiles_per_seq),
        grid=(T // tm,),
        in_specs=[
            pl.BlockSpec((tm, D_FF), lambda i: (i, 0)),
            pl.BlockSpec((tm, D_FF), lambda i: (i, 1)),
            pl.BlockSpec((FFN_HALO, D_FF), lambda i: (jnp.maximum(i * hb - 1, 0), 0)),
            pl.BlockSpec((FFN_HALO, D_FF), lambda i: (jnp.minimum((i + 1) * hb, last_hb), 0)),
            pl.BlockSpec((tm, D_MODEL), lambda i: (i, 0)),
            pl.BlockSpec((3, D_FF), lambda i: (0, 0)),
            pl.BlockSpec((1, D_FF), lambda i: (0, 0)),
            pl.BlockSpec((D_FF, D_MODEL), lambda i: (0, 0)),
            pl.BlockSpec((1, D_MODEL), lambda i: (0, 0)),
        ],
        out_specs=pl.BlockSpec((tm, D_MODEL), lambda i: (i, 0)),
        out_shape=jax.ShapeDtypeStruct((T, D_MODEL), F32),
        scratch_shapes=[pltpu.VMEM((tm, D_FF), BF16)],
        compiler_params=_params(("parallel",), 56),
    )(up, up, up, up, x1, conv_w, conv_b.reshape(1, D_FF), wd, norm_g.reshape(1, D_MODEL))


def _trunk(x, w):
    B, S, _ = x.shape
    T = B * S
    x2d = x.reshape(T, D_MODEL)
    proj, small = _norm_matmul(x2d, w["norm_mix_g"], w["w_in"], w["w_small"], tm=1024, tn=1536)
    proj3 = proj.reshape(B, S, PROJ_COLS)
    o_att = _attention(proj3, *_rope_tables(S))
    o_dn = _deltanet(proj3, small.reshape(B, S, LANES), w["conv_qkv_w"], w["dn_par"], w["out_norm_g"])
    x1 = _mix(x2d, o_att.reshape(T, ATT_GROUP_WIDTH), o_dn.reshape(T, DN_WIDTH), proj,
              w["w_branch_a"], w["w_branch_b"], w["w_out"], tm=1024)
    up = _norm_matmul(x1, w["norm_ffn_g"], w["w_up"], tm=1024, tn=D_FF)[0]
    y = _ffn_out(up, x1, w["ffn_conv_w"], w["ffn_conv_b"], w["w_down"], w["norm_final_g"], tm=512, S=S)
    return y.reshape(B, S, D_MODEL)


def kernel(x_prompt, x_sample, norm_mix_g, w_in, conv_qkv_w, a_log_f, a_log_b, dt_bias_f, dt_bias_b, out_norm_g, w_branch_a, w_branch_b, w_out, norm_ffn_g, w_up, ffn_conv_w, ffn_conv_b, w_down, norm_final_g):
    att = 3 * ATT_WIDTH
    dn_end = att + 3 * DN_WIDTH
    z_end = dn_end + DN_WIDTH
    small_end = z_end + 4 * DN_HEADS
    w_in_r = jnp.concatenate([
        w_in[:, small_end:],
        w_in[:, dn_end:z_end],
        w_in[:, att:dn_end],
        w_in[:, :ATT_WIDTH] * (ATT_HEAD_DIM ** -0.5),
        w_in[:, ATT_WIDTH:att],
    ], axis=1).astype(BF16)
    w_small = jnp.pad(w_in[:, z_end:small_end], ((0, 0), (0, LANES - 4 * DN_HEADS))).astype(BF16)
    par = jnp.zeros((8, LANES), F32)
    par = par.at[0, 2 * DN_HEADS:4 * DN_HEADS].set(jnp.concatenate([a_log_f, a_log_b]))
    par = par.at[1, 2 * DN_HEADS:4 * DN_HEADS].set(jnp.concatenate([dt_bias_f, dt_bias_b]))
    w = dict(
        norm_mix_g=norm_mix_g, w_in=w_in_r, w_small=w_small, conv_qkv_w=conv_qkv_w, dn_par=par, out_norm_g=out_norm_g,
        w_branch_a=w_branch_a.astype(BF16), w_branch_b=w_branch_b.astype(BF16), w_out=w_out.astype(BF16),
        norm_ffn_g=norm_ffn_g, w_up=w_up.astype(BF16), ffn_conv_w=ffn_conv_w, ffn_conv_b=ffn_conv_b,
        w_down=w_down.astype(BF16), norm_final_g=norm_final_g,
    )
    return _trunk(x_prompt, w), _trunk(x_sample, w)
```
